```python
import jax
import jax.numpy as jnp
from jax import lax
import numpy as np

D_MODEL = 2048
BATCH = 8
SEQ = 2048
DEPTH = 1
DEC_BATCH = 32
DEC_SEQ = 4
PAST_LEN = 16384
PAGE_SIZE = 128

N_HEADS = 16
HEAD_DIM = 64
ATTN_WIDTH = N_HEADS * HEAD_DIM
Q_BLOCK = 128
FORGET_BIAS_INIT = 8.0
FORGET_BIAS_NOISE = 0.5
POOL_WINDOWS = (2, 4, 8, 16)
N_POOL_GROUPS = len(POOL_WINDOWS)
POOL_WIDTH = D_MODEL // 2
POOL_GROUP_DIM = POOL_WIDTH // N_POOL_GROUPS
POOL_STATE = max(POOL_WINDOWS) - 1
IN_WIDTH = 3 * ATTN_WIDTH + N_HEADS + POOL_WIDTH + 2 * D_MODEL
N_EXPERTS = 256
TOP_K = 8
N_GROUPS = 8
TOPK_GROUPS = 4
EXPERT_DIM = D_MODEL // 4
SHARED_DIM = EXPERT_DIM
ROUTED_SCALE = 2.5
MOE_BLOCK = 128
RMS_EPS = 1e-6

kernel_name = "fox_pool_gated_moe_adaln_step"


def rmsnorm(x, g):
    xf = x.astype(jnp.float32)
    y = xf * lax.rsqrt(jnp.mean(xf * xf, axis=-1, keepdims=True) + RMS_EPS)
    return (y * g.astype(jnp.float32)).astype(x.dtype)


def adaln(c, w_ada, b_ada):
    mod = (jax.nn.silu(c) @ w_ada + b_ada)[:, None, :]
    return jnp.split(mod, 6, axis=-1)


def modulate(h, shift, scale):
    return h * (1 + scale) + shift


def project_in(h, w_in, b_f):
    B, T, _ = h.shape
    A, P = ATTN_WIDTH, POOL_WIDTH
    cuts = [A, 2 * A, 3 * A, 3 * A + N_HEADS, 3 * A + N_HEADS + P, 3 * A + N_HEADS + P + D_MODEL]
    q, k, v, f, u, ga, gb = jnp.split(h @ w_in, cuts, axis=-1)
    heads = lambda t: t.reshape(B, T, N_HEADS, HEAD_DIM)
    logf = jax.nn.log_sigmoid((f + b_f).astype(jnp.float32))
    return heads(q), heads(k), heads(v), logf, u, ga, gb


def fox_prompt(q, k, v, logf):
    B, S = q.shape[:2]
    scale = HEAD_DIM ** -0.5
    qf, kf, vf = q.astype(jnp.float32), k.astype(jnp.float32), v.astype(jnp.float32)
    pos = jnp.arange(S)

    def one_block(blk):
        q0 = blk * Q_BLOCK
        qb = lax.dynamic_slice_in_dim(qf, q0, Q_BLOCK, axis=1)
        lf = jnp.where(pos[None, :, None] < q0 + Q_BLOCK, logf, 0.0)
        r = (lax.cumsum(lf, axis=1, reverse=True) - lf).transpose(0, 2, 1)
        r_q = lax.dynamic_slice_in_dim(r, q0, Q_BLOCK, axis=2)
        s = jnp.einsum('bqhd,bkhd->bhqk', qb, kf) * scale + r[:, :, None, :] - r_q[..., None]
        qpos = q0 + jnp.arange(Q_BLOCK)
        s = jnp.where(pos[None, :] <= qpos[:, None], s, -jnp.inf)
        p = jax.nn.softmax(s, axis=-1)
        return jnp.einsum('bhqk,bkhd->bqhd', p, vf)

    o = lax.map(one_block, jnp.arange(S // Q_BLOCK))
    return o.transpose(1, 0, 2, 3, 4).reshape(B, S, ATTN_WIDTH).astype(q.dtype)


def fox_sample(q, k, v, logf, cache_k, cache_v, cache_logf, page_table, layer):
    DB, T = q.shape[:2]
    n_pages = page_table.shape[1]
    page = cache_k.shape[2]
    past = n_pages * page
    scale = HEAD_DIM ** -0.5
    lf_past = cache_logf[layer, page_table].reshape(DB, past, N_HEADS).astype(jnp.float32)
    lf_all = jnp.concatenate([lf_past, logf], axis=1)
    r = lax.cumsum(lf_all, axis=1, reverse=True) - lf_all
    r_q = r[:, past:].transpose(0, 2, 1)
    r_past = r[:, :past].reshape(DB, n_pages, page, N_HEADS).transpose(1, 0, 3, 2)
    qf = q.astype(jnp.float32)
    s = jnp.einsum('bqhd,bkhd->bhqk', qf, k.astype(jnp.float32)) * scale + r_q[:, :, None, :] - r_q[..., None]
    s = jnp.where(jnp.tril(jnp.ones((T, T), bool)), s, -jnp.inf)
    m = s.max(-1)
    p = jnp.exp(s - m[..., None])
    l = p.sum(-1)
    acc = jnp.einsum('bhqk,bkhd->bhqd', p, v.astype(jnp.float32))

    def page_step(carry, xs):
        m, l, acc = carry
        phys, r_k = xs
        kp = cache_k[layer, phys].astype(jnp.float32)
        vp = cache_v[layer, phys].astype(jnp.float32)
        s = jnp.einsum('bqhd,bkhd->bhqk', qf, kp) * scale + r_k[:, :, None, :] - r_q[..., None]
        m_new = jnp.maximum(m, s.max(-1))
        p = jnp.exp(s - m_new[..., None])
        corr = jnp.exp(m - m_new)
        acc = acc * corr[..., None] + jnp.einsum('bhqk,bkhd->bhqd', p, vp)
        return (m_new, l * corr + p.sum(-1), acc), None

    (m, l, acc), _ = lax.scan(page_step, (m, l, acc), (page_table.T, r_past))
    o = acc / l[..., None]
    return o.transpose(0, 2, 1, 3).reshape(DB, T, ATTN_WIDTH).astype(q.dtype)


def pool_mix(u, prev, w_pool, pool_scale):
    B, T, _ = u.shape
    if prev is None:
        prev = u[:, :0]
    ext = jnp.concatenate([prev, u], axis=1)
    p0 = prev.shape[1]
    maxw = max(POOL_WINDOWS)
    csum = jnp.concatenate([jnp.zeros((B, maxw, POOL_WIDTH), jnp.float32),
                            lax.cumsum(ext.astype(jnp.float32), axis=1)], axis=1)
    e = p0 + jnp.arange(T)
    groups = []
    for g, w in enumerate(POOL_WINDOWS):
        cols = slice(g * POOL_GROUP_DIM, (g + 1) * POOL_GROUP_DIM)
        hi = csum[:, maxw + p0: maxw + p0 + T, cols]
        lo = csum[:, maxw + p0 - w: maxw + p0 - w + T, cols]
        count = jnp.minimum(w, e + 1).astype(jnp.float32)[None, :, None]
        groups.append((hi - lo) / count)
    pooled = jnp.concatenate(groups, axis=-1) - u.astype(jnp.float32)
    mixed = jnp.einsum('btgc,gcd->btgd', pooled.reshape(B, T, N_POOL_GROUPS, POOL_GROUP_DIM),
                       w_pool.astype(jnp.float32)).reshape(B, T, POOL_WIDTH)
    return (mixed * pool_scale).astype(u.dtype), ext[:, -POOL_STATE:]


def merge_branches(o_att, o_pool, ga, gb, w_pa, w_pb, w_o):
    y = jax.nn.sigmoid(ga) * (o_att @ w_pa) + jax.nn.sigmoid(gb) * (o_pool @ w_pb)
    return y @ w_o


def swiglu(x, wg, wu, wd):
    return (jax.nn.silu(x @ wg) * (x @ wu)) @ wd


def routed_experts(xf, top_idx, top_w, w_gate, w_up, w_down, layer):
    T, D = xf.shape
    n_assign = T * TOP_K
    ratio = max(1, n_assign // N_EXPERTS)
    block = min(MOE_BLOCK, max(8, 1 << (ratio.bit_length() - 1)))
    n_blocks = -(-n_assign // block) + N_EXPERTS
    flat_e = top_idx.reshape(-1)
    flat_tok = jnp.arange(n_assign, dtype=jnp.int32) // TOP_K
    flat_w = top_w.reshape(-1)
    order = jnp.argsort(flat_e)
    e_sorted = flat_e[order]
    counts = jnp.bincount(flat_e, length=N_EXPERTS)
    padded = (counts + block - 1) // block * block
    pad_end = jnp.cumsum(padded)
    pad_start = pad_end - padded
    start = jnp.cumsum(counts) - counts
    dest = pad_start[e_sorted] + (jnp.arange(n_assign) - start[e_sorted])
    n_rows = n_blocks * block
    row_tok = jnp.full((n_rows,), T, jnp.int32).at[dest].set(flat_tok[order])
    row_w = jnp.zeros((n_rows,), jnp.float32).at[dest].set(flat_w[order])
    block_e = jnp.minimum(jnp.searchsorted(pad_end, jnp.arange(n_blocks) * block, side='right'), N_EXPERTS - 1)
    xpad = jnp.concatenate([xf, jnp.zeros((1, D), xf.dtype)], axis=0)

    def block_step(acc, blk):
        tok, wts, e = blk
        xb = xpad[tok]
        hb = jax.nn.silu(xb @ w_gate[layer, e]) * (xb @ w_up[layer, e])
        yb = (hb @ w_down[layer, e]).astype(jnp.float32) * wts[:, None]
        return acc.at[tok].add(yb), None

    acc, _ = lax.scan(block_step, jnp.zeros((T + 1, D), jnp.float32),
                      (row_tok.reshape(n_blocks, block), row_w.reshape(n_blocks, block), block_e))
    return acc[:T]


def moe_ffn(h, layer, w_router, router_bias, w_gate, w_up, w_down, w_sh_gate, w_sh_up, w_sh_down):
    B, T, D = h.shape
    xf = h.reshape(B * T, D)
    scores = jax.nn.sigmoid((xf @ w_router[layer]).astype(jnp.float32))
    sel = scores + router_bias[layer].astype(jnp.float32)
    grp = sel.reshape(-1, N_GROUPS, N_EXPERTS // N_GROUPS)
    gscore = lax.top_k(grp, 2)[0].sum(-1)
    _, gidx = lax.top_k(gscore, TOPK_GROUPS)
    gmask = jax.nn.one_hot(gidx, N_GROUPS, dtype=jnp.float32).sum(-2) > 0
    emask = jnp.repeat(gmask, N_EXPERTS // N_GROUPS, axis=-1)
    _, top_idx = lax.top_k(jnp.where(emask, sel, -jnp.inf), TOP_K)
    top_w = jnp.take_along_axis(scores, top_idx, axis=-1)
    top_w = top_w / top_w.sum(-1, keepdims=True) * ROUTED_SCALE
    routed = routed_experts(xf, top_idx, top_w, w_gate, w_up, w_down, layer)
    shared = swiglu(xf, w_sh_gate[layer], w_sh_up[layer], w_sh_down[layer]).astype(jnp.float32)
    return (routed + shared).astype(h.dtype).reshape(B, T, D)


def trunk_layer(x, c, attend, prev_pool, layer, w_ada, b_ada, g_mix, w_in, b_f, w_pool, pool_scale,
                w_pa, w_pb, w_o, g_ffn, w_router, router_bias, w_gate, w_up, w_down,
                w_sh_gate, w_sh_up, w_sh_down):
    sh1, sc1, gt1, sh2, sc2, gt2 = adaln(c, w_ada[layer], b_ada[layer])
    h = modulate(rmsnorm(x, g_mix[layer]), sh1, sc1)
    q, k, v, logf, u, ga, gb = project_in(h, w_in[layer], b_f[layer])
    o_att = attend(q, k, v, logf)
    o_pool, pool_buf = pool_mix(u, prev_pool, w_pool[layer], pool_scale[layer])
    x = x + gt1 * merge_branches(o_att, o_pool, ga, gb, w_pa[layer], w_pb[layer], w_o[layer])
    h = modulate(rmsnorm(x, g_ffn[layer]), sh2, sc2)
    x = x + gt2 * moe_ffn(h, layer, w_router, router_bias, w_gate, w_up, w_down, w_sh_gate, w_sh_up, w_sh_down)
    return x, k, v, logf, pool_buf


def setup_inputs(seed: int = 0) -> dict:
    key = jax.random.key(seed)
    ks = jax.random.split(key, 32)
    f32 = jnp.float32
    nrm = lambda k, shape, s=1.0: s * jax.random.normal(k, shape, f32)
    n_pages = PAST_LEN // PAGE_SIZE
    n_used = DEC_BATCH * n_pages
    n_pool_pages = (5 * n_used + 3) // 4
    page_table = jax.random.permutation(ks[0], n_pool_pages)[:n_used].reshape(DEC_BATCH, n_pages).astype(jnp.int32)
    D, A, P, E, F = D_MODEL, ATTN_WIDTH, POOL_WIDTH, N_EXPERTS, EXPERT_DIM
    return {
        "x_prompt": nrm(ks[1], (BATCH, SEQ, D)),
        "x_sample": nrm(ks[2], (DEC_BATCH, DEC_SEQ, D)),
        "cache_k": nrm(ks[3], (DEPTH, n_pool_pages, PAGE_SIZE, N_HEADS, HEAD_DIM)),
        "cache_v": nrm(ks[4], (DEPTH, n_pool_pages, PAGE_SIZE, N_HEADS, HEAD_DIM)),
        "cache_logf": jax.nn.log_sigmoid(FORGET_BIAS_INIT + nrm(ks[5], (DEPTH, n_pool_pages, PAGE_SIZE, N_HEADS),
                                                                 FORGET_BIAS_NOISE)),
        "state_pool": nrm(ks[6], (DEPTH, DEC_BATCH, POOL_STATE, P)),
        "page_table": page_table,
        "c_prompt": nrm(ks[7], (BATCH, D)),
        "c_sample": nrm(ks[8], (DEC_BATCH, D)),
        "w_ada": nrm(ks[9], (DEPTH, D, 6 * D), 0.3 * D ** -0.5),
        "b_ada": nrm(ks[10], (DEPTH, 6 * D), 0.02),
        "g_mix": 1.0 + nrm(ks[11], (DEPTH, D), 0.02),
        "w_in": nrm(ks[12], (DEPTH, D, IN_WIDTH), D ** -0.5),
        "b_f": FORGET_BIAS_INIT + nrm(ks[13], (DEPTH, N_HEADS), FORGET_BIAS_NOISE),
        "w_pool": nrm(ks[14], (DEPTH, N_POOL_GROUPS, POOL_GROUP_DIM, POOL_GROUP_DIM), POOL_GROUP_DIM ** -0.5),
        "pool_scale": 1.0 + nrm(ks[15], (DEPTH, P), 0.1),
        "w_pa": nrm(ks[16], (DEPTH, A, D), A ** -0.5),
        "w_pb": nrm(ks[17], (DEPTH, P, D), P ** -0.5),
        "w_o": nrm(ks[18], (DEPTH, D, D), D ** -0.5),
        "g_ffn": 1.0 + nrm(ks[19], (DEPTH, D), 0.02),
        "w_router": nrm(ks[20], (DEPTH, D, E), D ** -0.5),
        "router_bias": nrm(ks[21], (DEPTH, E), 0.01),
        "w_gate": nrm(ks[22], (DEPTH, E, D, F), D ** -0.5),
        "w_up": nrm(ks[23], (DEPTH, E, D, F), D ** -0.5),
        "w_down": nrm(ks[24], (DEPTH, E, F, D), F ** -0.5),
        "w_sh_gate": nrm(ks[25], (DEPTH, D, SHARED_DIM), D ** -0.5),
        "w_sh_up": nrm(ks[26], (DEPTH, D, SHARED_DIM), D ** -0.5),
        "w_sh_down": nrm(ks[27], (DEPTH, SHARED_DIM, D), SHARED_DIM ** -0.5),
        "g_final": 1.0 + nrm(ks[28], (D,), 0.02),
    }


def reference(x_prompt, x_sample, cache_k, cache_v, cache_logf, state_pool, page_table, c_prompt, c_sample,
              w_ada, b_ada, g_mix, w_in, b_f, w_pool, pool_scale, w_pa, w_pb, w_o, g_ffn,
              w_router, router_bias, w_gate, w_up, w_down, w_sh_gate, w_sh_up, w_sh_down, g_final):
    params = (w_ada, b_ada, g_mix, w_in, b_f, w_pool, pool_scale, w_pa, w_pb, w_o, g_ffn,
              w_router, router_bias, w_gate, w_up, w_down, w_sh_gate, w_sh_up, w_sh_down)
    xp, xs = x_prompt, x_sample
    kp_l, vp_l, fp_l, bp_l, ks_l, vs_l, fs_l, bs_l = [], [], [], [], [], [], [], []
    for layer in range(DEPTH):
        attend_sample = lambda q, k, v, lf, layer=layer: fox_sample(q, k, v, lf, cache_k, cache_v, cache_logf,
                                                                    page_table, layer)
        xp, kp, vp, fp, bp = trunk_layer(xp, c_prompt, fox_prompt, None, layer, *params)
        xs, ks, vs, fs, bs = trunk_layer(xs, c_sample, attend_sample, state_pool[layer], layer, *params)
        kp_l.append(kp); vp_l.append(vp); fp_l.append(fp); bp_l.append(bp)
        ks_l.append(ks); vs_l.append(vs); fs_l.append(fs); bs_l.append(bs)
    y_prompt = rmsnorm(xp, g_final)
    y_sample = rmsnorm(xs, g_final)
    return (y_prompt, y_sample, jnp.stack(kp_l), jnp.stack(vp_l), jnp.stack(fp_l), jnp.stack(bp_l),
            jnp.stack(ks_l), jnp.stack(vs_l), jnp.stack(fs_l), jnp.stack(bs_l))
```

```python
import functools

import jax
import jax.numpy as jnp
from jax import lax
from jax.experimental import pallas as pl
from jax.experimental.pallas import tpu as pltpu

F32 = jnp.float32
BF16 = jnp.bfloat16

RMS_EPS = 1e-6
TOP_K = 8
N_GROUPS = 8
TOPK_GROUPS = 4
ROUTED_SCALE = 2.5
POOL_WINDOWS = (2, 4, 8, 16)
POOL_HALO = 16
LANES = 128
MOE_BLOCK = 128
VMEM_LIMIT = 56 * 1024 * 1024

_dot = functools.partial(jnp.dot, preferred_element_type=F32)


def _dot_nt(a, b, precision=None):
    return lax.dot_general(a, b, (((1,), (1,)), ((), ())), preferred_element_type=F32, precision=precision)


def _params(*sem):
    return pltpu.CompilerParams(dimension_semantics=sem, vmem_limit_bytes=VMEM_LIMIT)


def _vmem():
    return pl.BlockSpec(memory_space=pltpu.VMEM)


def _rms(x, g):
    return x * lax.rsqrt(jnp.mean(x * x, axis=-1, keepdims=True) + RMS_EPS) * g


def _rms_mod(x, g, shift, scale):
    return _rms(x, g) * (1.0 + scale) + shift


def _log_sigmoid(x):
    return jnp.minimum(x, 0.0) - jnp.log1p(jnp.exp(-jnp.abs(x)))


def _silu(x):
    return x * jax.nn.sigmoid(x)


def _cumsum_lanes(x):
    n = x.shape[-1]
    lane = lax.broadcasted_iota(jnp.int32, x.shape, x.ndim - 1)
    k = 1
    while k < n:
        x = x + jnp.where(lane >= k, pltpu.roll(x, k, axis=x.ndim - 1), 0.0)
        k *= 2
    return x


def _mod_spec(mod, tm, nt):
    _, r, d = mod.shape
    if r == 1:
        return pl.BlockSpec((1, 1, d), lambda b, i: (b, 0, 0))
    return pl.BlockSpec((1, tm, d), lambda b, i: (b, i, 0))


def _ada_kernel(c_ref, w_ref, b_ref, o_ref):
    a = _silu(c_ref[...]).astype(BF16)
    o_ref[...] = _dot(a, w_ref[...].astype(BF16)) + b_ref[...]


def _ada(c, w, b):
    r, d = c.shape
    n = w.shape[1]
    tn = min(n, 1024)
    return pl.pallas_call(
        _ada_kernel,
        grid=(n // tn,),
        in_specs=[pl.BlockSpec((r, d), lambda j: (0, 0)),
                  pl.BlockSpec((d, tn), lambda j: (0, j)),
                  pl.BlockSpec((1, tn), lambda j: (0, j))],
        out_specs=pl.BlockSpec((r, tn), lambda j: (0, j)),
        out_shape=jax.ShapeDtypeStruct((r, n), F32),
        compiler_params=_params("parallel"),
        name="ada",
    )(c, w, b.reshape(1, n))


def _inproj_kernel(x_ref, g_ref, sh_ref, sc_ref, wq_ref, wk_ref, wv_ref, wu_ref, wf_ref, bf_ref,
                   q_ref, k_ref, v_ref, u_ref, lf_ref, *, qscale):
    h = _rms_mod(x_ref[0], g_ref[...], sh_ref[0], sc_ref[0]).astype(BF16)
    q_ref[0] = (_dot(h, wq_ref[...]) * qscale).astype(BF16)
    k_ref[0] = _dot(h, wk_ref[...])
    v_ref[0] = _dot(h, wv_ref[...])
    u_ref[0] = _dot(h, wu_ref[...])
    lf_ref[0] = _log_sigmoid(_dot(h, wf_ref[...]) + bf_ref[...])


def _inproj(x, g, shift, scale, wq, wk, wv, wu, wf, bf, qscale):
    bsz, s, d = x.shape
    a, p = wq.shape[1], wu.shape[1]
    tm = min(s, 512)
    nt = s // tm
    row = lambda n: pl.BlockSpec((1, tm, n), lambda b, i: (b, i, 0))
    return pl.pallas_call(
        functools.partial(_inproj_kernel, qscale=qscale),
        grid=(bsz, nt),
        in_specs=[row(d), pl.BlockSpec((1, d), lambda b, i: (0, 0)),
                  _mod_spec(shift, tm, nt), _mod_spec(scale, tm, nt),
                  _vmem(), _vmem(), _vmem(), _vmem(), _vmem(), _vmem()],
        out_specs=[row(a), row(a), row(a), row(p), row(LANES)],
        out_shape=[jax.ShapeDtypeStruct((bsz, s, a), BF16),
                   jax.ShapeDtypeStruct((bsz, s, a), F32),
                   jax.ShapeDtypeStruct((bsz, s, a), F32),
                   jax.ShapeDtypeStruct((bsz, s, p), F32),
                   jax.ShapeDtypeStruct((bsz, s, LANES), F32)],
        compiler_params=_params("parallel", "parallel"),
        name="inproj",
    )(x, g, shift, scale, wq, wk, wv, wu, wf, bf)


def _cumsum_kernel(x_ref, o_ref):
    o_ref[0] = _cumsum_lanes(x_ref[0])


def _cumsum_rows(x):
    bsz, h, s = x.shape
    spec = pl.BlockSpec((1, h, s), lambda b: (b, 0, 0))
    return pl.pallas_call(
        _cumsum_kernel, grid=(bsz,), in_specs=[spec], out_specs=spec,
        out_shape=jax.ShapeDtypeStruct(x.shape, F32),
        compiler_params=_params("parallel"), name="logf_cumsum",
    )(x)


def _attn_prompt_kernel(q_ref, k_ref, v_ref, c_ref, o_ref, m_ref, l_ref, acc_ref, *, t, dh):
    qi = pl.program_id(2)
    q = q_ref[0]
    lane = lax.broadcasted_iota(jnp.int32, (1, 2 * dh), 1)
    second = lane >= dh
    zero = jnp.zeros_like(q)
    qh = (jnp.where(second, zero, q), jnp.where(second, q, zero))
    m_ref[...] = jnp.full(m_ref.shape, -jnp.inf, F32)
    l_ref[...] = jnp.zeros(l_ref.shape, F32)
    acc_ref[...] = jnp.zeros(acc_ref.shape, F32)

    def tile(ki, diagonal):
        k0 = pl.multiple_of(ki * t, t)
        kb = k_ref[0, pl.ds(k0, t), :].astype(BF16)
        vb = v_ref[0, pl.ds(k0, t), :].astype(BF16)
        cb = c_ref[0, 0, ki]
        pv, corr = [], []
        for j in range(2):
            s = _dot_nt(qh[j], kb) - cb[j:j + 1, :]
            if diagonal:
                rowi = lax.broadcasted_iota(jnp.int32, (t, t), 0)
                coli = lax.broadcasted_iota(jnp.int32, (t, t), 1)
                s = jnp.where(coli <= rowi, s, -jnp.inf)
            m_old = m_ref[j]
            m_new = jnp.maximum(m_old, jnp.max(s, axis=-1, keepdims=True))
            p = jnp.exp(s - m_new)
            cj = jnp.exp(m_old - m_new)
            l_ref[j] = l_ref[j] * cj + jnp.sum(p, axis=-1, keepdims=True)
            m_ref[j] = m_new
            pv.append(_dot(p.astype(BF16), vb))
            corr.append(cj)
        acc_ref[...] = acc_ref[...] * jnp.where(second, corr[1], corr[0]) + jnp.where(second, pv[1], pv[0])

    def body(ki, carry):
        tile(ki, False)
        return carry

    lax.fori_loop(0, qi, body, 0)
    tile(qi, True)
    o_ref[0] = (acc_ref[...] / jnp.where(second, l_ref[1], l_ref[0])).astype(o_ref.dtype)


def _attn_prompt(q, k, v, cum, dh):
    bsz, s, a = q.shape
    hp = a // (2 * dh)
    t = min(s, 512)
    nt = s // t
    cum = cum.reshape(bsz, hp, 2, nt, t).transpose(0, 1, 3, 2, 4)
    return pl.pallas_call(
        functools.partial(_attn_prompt_kernel, t=t, dh=dh),
        grid=(bsz, hp, nt),
        in_specs=[pl.BlockSpec((1, t, 2 * dh), lambda b, h, i: (b, i, h)),
                  pl.BlockSpec((1, s, 2 * dh), lambda b, h, i: (b, 0, h)),
                  pl.BlockSpec((1, s, 2 * dh), lambda b, h, i: (b, 0, h)),
                  pl.BlockSpec((1, 1, nt, 2, t), lambda b, h, i: (b, h, 0, 0, 0))],
        out_specs=pl.BlockSpec((1, t, 2 * dh), lambda b, h, i: (b, i, h)),
        out_shape=jax.ShapeDtypeStruct((bsz, s, a), BF16),
        scratch_shapes=[pltpu.VMEM((2, t, 1), F32), pltpu.VMEM((2, t, 1), F32),
                        pltpu.VMEM((t, 2 * dh), F32)],
        compiler_params=_params("parallel", "parallel", "parallel"),
        name="attn_prompt",
    )(q, k, v, cum)


def _decode_kernel(pt_ref, qb_ref, kn_ref, vn_ref, lfn_ref, *refs, g_pages, n_new, n_heads, dh):
    k_refs = refs[:g_pages]
    v_refs = refs[g_pages:2 * g_pages]
    lf_refs = refs[2 * g_pages:3 * g_pages]
    o_ref, m_ref, l_ref, acc_ref, carry_ref = refs[3 * g_pages:]
    step = pl.program_id(1)
    rows = n_new * n_heads
    qb = qb_ref[0]

    @pl.when(step == 0)
    def _():
        c = _cumsum_lanes(lfn_ref[0])
        tot = c[:, LANES - 1:LANES]
        rnew = (tot - c)[:, :16]
        s = _dot_nt(qb, kn_ref[0]) + jnp.concatenate([rnew] * n_new, axis=0)
        tok = lax.broadcasted_iota(jnp.int32, s.shape, 0) // n_heads
        key = lax.broadcasted_iota(jnp.int32, s.shape, 1)
        s = jnp.where(key <= tok, s, -jnp.inf)
        m = jnp.max(s, axis=-1, keepdims=True)
        p = jnp.exp(s - m)
        m_ref[...] = m
        l_ref[...] = jnp.sum(p, axis=-1, keepdims=True)
        acc_ref[...] = _dot(p.astype(BF16), vn_ref[0])
        carry_ref[...] = jnp.broadcast_to(tot, carry_ref.shape)

    for g in range(g_pages):
        c = _cumsum_lanes(lf_refs[g][0])
        tot = c[:, c.shape[1] - 1:]
        carry = carry_ref[...]
        r = carry + (tot - c)
        carry_ref[...] = carry + tot
        kb = k_refs[g][0].astype(BF16)
        vb = v_refs[g][0].astype(BF16)
        s = _dot_nt(qb, kb) + jnp.concatenate([r] * n_new, axis=0)
        m_old = m_ref[...]
        m_new = jnp.maximum(m_old, jnp.max(s, axis=-1, keepdims=True))
        p = jnp.exp(s - m_new)
        corr = jnp.exp(m_old - m_new)
        l_ref[...] = l_ref[...] * corr + jnp.sum(p, axis=-1, keepdims=True)
        m_ref[...] = m_new
        acc_ref[...] = acc_ref[...] * corr + _dot(p.astype(BF16), vb)

    @pl.when(step == pl.num_programs(1) - 1)
    def _():
        o = acc_ref[...] / l_ref[...]
        head = lax.broadcasted_iota(jnp.int32, o.shape, 0) % n_heads
        lane_head = lax.broadcasted_iota(jnp.int32, o.shape, 1) // dh
        o = jnp.where(head == lane_head, o, 0.0)
        o_ref[0] = jnp.sum(o.reshape(n_new, n_heads, o.shape[1]), axis=1)


def _decode_attn(q, k, v, logf, cache_k, cache_v, cache_logf_t, page_table, n_heads, dh):
    db, t, a = q.shape
    n_pages = page_table.shape[1]
    page = cache_k.shape[1]
    assert page == LANES and t <= 16
    g_pages = 8 if n_pages % 8 == 0 else (4 if n_pages % 4 == 0 else 1)
    steps = n_pages // g_pages
    rows = t * n_heads
    eye = jnp.eye(n_heads, dtype=BF16)
    qb = (q.reshape(db, t, 1, n_heads, dh) * eye[None, None, :, :, None]).reshape(db, rows, a)
    pad = lambda z: jnp.pad(z.astype(BF16), ((0, 0), (0, 16 - t), (0, 0)))
    lfn = jnp.pad(logf.transpose(0, 2, 1), ((0, 0), (0, 0), (0, LANES - t)))

    def page_idx(g):
        return lambda b, s, pt: (pt[b * n_pages + n_pages - 1 - (s * g_pages + g)], 0, 0)

    seq = lambda shape: pl.BlockSpec((1,) + shape, lambda b, s, pt: (b, 0, 0))
    in_specs = [seq((rows, a)), seq((16, a)), seq((16, a)), seq((n_heads, LANES))]
    in_specs += [pl.BlockSpec((1, page, a), page_idx(g)) for g in range(g_pages)]
    in_specs += [pl.BlockSpec((1, page, a), page_idx(g)) for g in range(g_pages)]
    in_specs += [pl.BlockSpec((1, n_heads, page), page_idx(g)) for g in range(g_pages)]
    grid_spec = pltpu.PrefetchScalarGridSpec(
        num_scalar_prefetch=1, grid=(db, steps), in_specs=in_specs,
        out_specs=seq((t, a)),
        scratch_shapes=[pltpu.VMEM((rows, 1), F32), pltpu.VMEM((rows, 1), F32),
                        pltpu.VMEM((rows, a), F32), pltpu.VMEM((n_heads, LANES), F32)])
    return pl.pallas_call(
        functools.partial(_decode_kernel, g_pages=g_pages, n_new=t, n_heads=n_heads, dh=dh),
        grid_spec=grid_spec,
        out_shape=jax.ShapeDtypeStruct((db, t, a), F32),
        compiler_params=_params("parallel", "arbitrary"),
        name="attn_decode",
    )(page_table.reshape(-1), qb, pad(k), pad(v), lfn,
      *([cache_k] * g_pages), *([cache_v] * g_pages), *([cache_logf_t] * g_pages))


def _pool_prompt_kernel(u_ref, halo_ref, o_ref, *, tm, pg):
    i = pl.program_id(1)
    u = u_ref[0]
    halo = jnp.where(i > 0, halo_ref[0], 0.0)
    ext = jnp.concatenate([halo, u], axis=0)
    pos = i * tm + lax.broadcasted_iota(jnp.int32, (tm, 1), 0)
    outs = []
    for g, w in enumerate(POOL_WINDOWS):
        r = ext[:, g * pg:(g + 1) * pg]
        k = 1
        while k < w:
            r = r + pltpu.roll(r, k, axis=0)
            k *= 2
        count = jnp.minimum(w, pos + 1).astype(F32)
        outs.append(r[POOL_HALO:] / count - u[:, g * pg:(g + 1) * pg])
    o_ref[0] = jnp.concatenate(outs, axis=-1).astype(o_ref.dtype)


def _pool_prompt(u):
    bsz, s, p = u.shape
    tm = min(s, 512)
    hb = tm // POOL_HALO
    return pl.pallas_call(
        functools.partial(_pool_prompt_kernel, tm=tm, pg=p // len(POOL_WINDOWS)),
        grid=(bsz, s // tm),
        in_specs=[pl.BlockSpec((1, tm, p), lambda b, i: (b, i, 0)),
                  pl.BlockSpec((1, POOL_HALO, p), lambda b, i: (b, jnp.maximum(i * hb - 1, 0), 0))],
        out_specs=pl.BlockSpec((1, tm, p), lambda b, i: (b, i, 0)),
        out_shape=jax.ShapeDtypeStruct(u.shape, BF16),
        compiler_params=_params("parallel", "parallel"),
        name="pool_prompt",
    )(u, u)


def _pool_sample_kernel(e_ref, o_ref, *, n_new, n_prev, pg):
    for t in range(n_new):
        e = n_prev + t
        outs = []
        for g, w in enumerate(POOL_WINDOWS):
            cols = slice(g * pg, (g + 1) * pg)
            lo = max(0, e - w + 1)
            acc = e_ref[lo, :, cols]
            for j in range(lo + 1, e + 1):
                acc = acc + e_ref[j, :, cols]
            outs.append(acc / float(min(w, e + 1)) - e_ref[e, :, cols])
        o_ref[t] = jnp.concatenate(outs, axis=-1).astype(o_ref.dtype)


def _pool_sample(ext_t, n_new):
    n_all, db, p = ext_t.shape
    return pl.pallas_call(
        functools.partial(_pool_sample_kernel, n_new=n_new, n_prev=n_all - n_new, pg=p // len(POOL_WINDOWS)),
        in_specs=[_vmem()], out_specs=_vmem(),
        out_shape=jax.ShapeDtypeStruct((n_new, db, p), BF16),
        compiler_params=pltpu.CompilerParams(vmem_limit_bytes=VMEM_LIMIT),
        name="pool_sample",
    )(ext_t)


def _merge_kernel(x_ref, g_ref, sh_ref, sc_ref, gt_ref, oa_ref, pl_ref, wpool_ref, ps_ref,
                  wpa_ref, wpb_ref, wga_ref, wgb_ref, wo_ref, o_ref):
    x = x_ref[0]
    h = _rms_mod(x, g_ref[...], sh_ref[0], sc_ref[0]).astype(BF16)
    pooled = pl_ref[0]
    n_groups, pg, _ = wpool_ref.shape
    mixed = jnp.concatenate(
        [_dot(pooled[:, g * pg:(g + 1) * pg], wpool_ref[g]) for g in range(n_groups)], axis=-1)
    o_pool = (mixed * ps_ref[...]).astype(BF16)
    y = jax.nn.sigmoid(_dot(h, wga_ref[...])) * _dot(oa_ref[0], wpa_ref[...])
    y = y + jax.nn.sigmoid(_dot(h, wgb_ref[...])) * _dot(o_pool, wpb_ref[...])
    o_ref[0] = x + gt_ref[0] * _dot(y.astype(BF16), wo_ref[...])


def _merge(x, g, shift, scale, gate, o_att, pooled, w_pool, pool_scale, w_pa, w_pb, w_ga, w_gb, w_o):
    bsz, s, d = x.shape
    tm = min(s, 256)
    nt = s // tm
    row = lambda n: pl.BlockSpec((1, tm, n), lambda b, i: (b, i, 0))
    return pl.pallas_call(
        _merge_kernel,
        grid=(bsz, nt),
        in_specs=[row(d), pl.BlockSpec((1, d), lambda b, i: (0, 0)),
                  _mod_spec(shift, tm, nt), _mod_spec(scale, tm, nt), _mod_spec(gate, tm, nt),
                  row(o_att.shape[2]), row(pooled.shape[2]),
                  _vmem(), _vmem(), _vmem(), _vmem(), _vmem(), _vmem(), _vmem()],
        out_specs=row(d),
        out_shape=jax.ShapeDtypeStruct(x.shape, F32),
        compiler_params=_params("parallel", "parallel"),
        name="merge",
    )(x, g, shift, scale, gate, o_att, pooled, w_pool, pool_scale, w_pa, w_pb, w_ga, w_gb, w_o)


def _route_kernel(x_ref, g_ref, sh_ref, sc_ref, wr_ref, rb_ref, h_ref, idx_ref, w_ref):
    h = _rms_mod(x_ref[0], g_ref[...], sh_ref[0], sc_ref[0])
    h_ref[0] = h.astype(BF16)
    n_exp = wr_ref.shape[0]
    tm = h.shape[0]
    per = n_exp // N_GROUPS
    scores = jax.nn.sigmoid(_dot_nt(wr_ref[...], h, precision=lax.Precision.HIGHEST))
    sel = scores + rb_ref[...]
    neg = -jnp.inf
    groups = [sel[g * per:(g + 1) * per] for g in range(N_GROUPS)]
    eidx = lax.broadcasted_iota(jnp.int32, (per, tm), 0)
    gscore = []
    for grp in groups:
        m1 = jnp.max(grp, axis=0, keepdims=True)
        first = jnp.min(jnp.where(grp == m1, eidx, per), axis=0, keepdims=True)
        gscore.append(m1 + jnp.max(jnp.where(eidx == first, neg, grp), axis=0, keepdims=True))
    cand = []
    for g in range(N_GROUPS):
        rank = jnp.zeros((1, tm), jnp.int32)
        for o in range(N_GROUPS):
            if o != g:
                ahead = gscore[o] >= gscore[g] if o < g else gscore[o] > gscore[g]
                rank = rank + ahead.astype(jnp.int32)
        cand.append(jnp.where(rank < TOPK_GROUPS, groups[g], neg))
    cand = jnp.concatenate(cand, axis=0)
    row = lax.broadcasted_iota(jnp.int32, (n_exp, tm), 0)
    idxs, wts = [], []
    for _ in range(TOP_K):
        m = jnp.max(cand, axis=0, keepdims=True)
        pick = jnp.min(jnp.where(cand == m, row, n_exp), axis=0, keepdims=True)
        hit = row == pick
        idxs.append(pick)
        wts.append(jnp.sum(jnp.where(hit, scores, 0.0), axis=0, keepdims=True))
        cand = jnp.where(hit, neg, cand)
    wts = jnp.concatenate(wts, axis=0)
    idx_ref[...] = jnp.concatenate(idxs, axis=0)
    w_ref[...] = wts / jnp.sum(wts, axis=0, keepdims=True) * ROUTED_SCALE


def _route(x, g, shift, scale, w_router_t, router_bias):
    bsz, s, d = x.shape
    tm = min(s, 256)
    nt = s // tm
    col = pl.BlockSpec((TOP_K, tm), lambda b, i: (0, b * nt + i))
    return pl.pallas_call(
        _route_kernel,
        grid=(bsz, nt),
        in_specs=[pl.BlockSpec((1, tm, d), lambda b, i: (b, i, 0)), pl.BlockSpec((1, d), lambda b, i: (0, 0)),
                  _mod_spec(shift, tm, nt), _mod_spec(scale, tm, nt), _vmem(), _vmem()],
        out_specs=[pl.BlockSpec((1, tm, d), lambda b, i: (b, i, 0)), col, col],
        out_shape=[jax.ShapeDtypeStruct(x.shape, BF16),
                   jax.ShapeDtypeStruct((TOP_K, bsz * s), jnp.int32),
                   jax.ShapeDtypeStruct((TOP_K, bsz * s), F32)],
        compiler_params=_params("parallel", "parallel"),
        name="route",
    )(x, g, shift, scale, w_router_t, router_bias)


def _moe_kernel(be_ref, nu_ref, xs_ref, wt_ref, wg_ref, wu_ref, wd_ref, o_ref, wgb, wub, wdb):
    i = pl.program_id(0)
    e = be_ref[i]
    prev = be_ref[jnp.maximum(i - 1, 0)]

    @pl.when((i == 0) | (e != prev))
    def _():
        wgb[...] = wg_ref[0].astype(BF16)
        wub[...] = wu_ref[0].astype(BF16)
        wdb[...] = wd_ref[0].astype(BF16)

    @pl.when(i < nu_ref[0])
    def _():
        xb = xs_ref[...]
        hid = (_silu(_dot(xb, wgb[...])) * _dot(xb, wub[...])).astype(BF16)
        o_ref[...] = _dot(hid, wdb[...]) * wt_ref[...]

    @pl.when(i >= nu_ref[0])
    def _():
        o_ref[...] = jnp.zeros(o_ref.shape, F32)


def _moe(xs, row_w, block_e, n_used, w_gate, w_up, w_down):
    n_rows, d = xs.shape
    f = w_gate.shape[2]
    n_blocks = n_rows // MOE_BLOCK
    grid_spec = pltpu.PrefetchScalarGridSpec(
        num_scalar_prefetch=2, grid=(n_blocks,),
        in_specs=[pl.BlockSpec((MOE_BLOCK, d), lambda i, be, nu: (i, 0)),
                  pl.BlockSpec((MOE_BLOCK, 1), lambda i, be, nu: (i, 0)),
                  pl.BlockSpec((1, d, f), lambda i, be, nu: (be[i], 0, 0)),
                  pl.BlockSpec((1, d, f), lambda i, be, nu: (be[i], 0, 0)),
                  pl.BlockSpec((1, f, d), lambda i, be, nu: (be[i], 0, 0))],
        out_specs=pl.BlockSpec((MOE_BLOCK, d), lambda i, be, nu: (i, 0)),
        scratch_shapes=[pltpu.VMEM((d, f), BF16), pltpu.VMEM((d, f), BF16), pltpu.VMEM((f, d), BF16)])
    return pl.pallas_call(
        _moe_kernel, grid_spec=grid_spec,
        out_shape=jax.ShapeDtypeStruct((n_rows, d), F32),
        compiler_params=_params("arbitrary"),
        name="moe",
    )(block_e, n_used, xs, row_w, w_gate, w_up, w_down)


def _final_kernel(x_ref, h_ref, r_ref, gt_ref, wg_ref, wu_ref, wd_ref, gf_ref, o_ref):
    hb = h_ref[0]
    hid = (_silu(_dot(hb, wg_ref[...])) * _dot(hb, wu_ref[...])).astype(BF16)
    moe = r_ref[...] + _dot(hid, wd_ref[...])
    o_ref[0] = _rms(x_ref[0] + gt_ref[0] * moe, gf_ref[...])


def _final(x, h, routed, row0, gate, w_sg, w_su, w_sd, g_final):
    bsz, s, d = x.shape
    tm = min(s, 256)
    nt = s // tm
    assert row0 % tm == 0
    rb0 = row0 // tm
    row = pl.BlockSpec((1, tm, d), lambda b, i: (b, i, 0))
    return pl.pallas_call(
        _final_kernel,
        grid=(bsz, nt),
        in_specs=[row, row, pl.BlockSpec((tm, d), lambda b, i: (rb0 + b * nt + i, 0)),
                  _mod_spec(gate, tm, nt), _vmem(), _vmem(), _vmem(), pl.BlockSpec((1, d), lambda b, i: (0, 0))],
        out_specs=row,
        out_shape=jax.ShapeDtypeStruct(x.shape, F32),
        compiler_params=_params("parallel", "parallel"),
        name="final",
    )(x, h, routed, gate, w_sg, w_su, w_sd, g_final)


def kernel(x_prompt, x_sample, cache_k, cache_v, cache_logf, state_pool, page_table, c_prompt, c_sample, w_ada, b_ada, g_mix, w_in, b_f, w_pool, pool_scale, w_pa, w_pb, w_o, g_ffn, w_router, router_bias, w_gate, w_up, w_down, w_sh_gate, w_sh_up, w_sh_down, g_final):
    depth = w_ada.shape[0]
    assert depth == 1
    bsz, seq, d = x_prompt.shape
    db, t_new, _ = x_sample.shape
    _, n_pool_pages, page, n_heads, dh = cache_k.shape
    a = n_heads * dh
    p = state_pool.shape[3]
    n_exp = w_router.shape[2]
    n_dec = db * t_new
    qscale = dh ** -0.5

    w = w_in[0]
    cuts = [a, 2 * a, 3 * a, 3 * a + n_heads, 3 * a + n_heads + p, 3 * a + n_heads + p + d]
    wq, wk, wv, wf, wu, wga, wgb = [z.astype(BF16) for z in jnp.split(w, cuts, axis=1)]
    wf = jnp.pad(wf, ((0, 0), (0, LANES - n_heads)))
    bf = jnp.pad(b_f[0], (0, LANES - n_heads)).reshape(1, LANES)
    wpool, wpa, wpb, wo = w_pool[0].astype(BF16), w_pa[0].astype(BF16), w_pb[0].astype(BF16), w_o[0].astype(BF16)
    wsg, wsu, wsd = w_sh_gate[0].astype(BF16), w_sh_up[0].astype(BF16), w_sh_down[0].astype(BF16)
    wr_t = w_router[0].T
    rbias = router_bias[0].reshape(n_exp, 1)
    gmix, gffn, gfin = g_mix[0].reshape(1, d), g_ffn[0].reshape(1, d), g_final.reshape(1, d)
    pscale = pool_scale[0].reshape(1, p)

    c_all = jnp.concatenate([c_prompt, c_sample], axis=0)
    r_pad = -c_all.shape[0] % 16
    mod = _ada(jnp.pad(c_all, ((0, r_pad), (0, 0))), w_ada[0], b_ada[0])
    mod_p = [m.reshape(bsz, 1, d) for m in jnp.split(mod[:bsz], 6, axis=-1)]
    mod_s = [jnp.repeat(m, t_new, axis=0).reshape(1, n_dec, d) for m in jnp.split(mod[bsz:bsz + db], 6, axis=-1)]

    def mixer_in(x, m):
        return _inproj(x, gmix, m[0], m[1], wq, wk, wv, wu, wf, bf, qscale)

    def mixer_out(x, m, o_att, pooled):
        x1 = _merge(x, gmix, m[0], m[1], m[2], o_att, pooled, wpool, pscale, wpa, wpb, wga, wgb, wo)
        return (x1,) + tuple(_route(x1, gffn, m[3], m[4], wr_t, rbias))

    q_p, k_p, v_p, u_p, lf_p = mixer_in(x_prompt, mod_p)
    logf_p = lf_p[:, :, :n_heads]
    cum_p = _cumsum_rows(logf_p.transpose(0, 2, 1))
    o_att_p = _attn_prompt(q_p, k_p, v_p, cum_p, dh)
    x1_p, h2_p, idx_p, wt_p = mixer_out(x_prompt, mod_p, o_att_p, _pool_prompt(u_p))

    xs3 = x_sample.reshape(1, n_dec, d)
    q_s, k_s, v_s, u_s, lf_s = mixer_in(xs3, mod_s)
    logf_s = lf_s[0, :, :n_heads].reshape(db, t_new, n_heads)
    u_s = u_s.reshape(db, t_new, p)
    o_att_s = _decode_attn(q_s.reshape(db, t_new, a), k_s.reshape(db, t_new, a), v_s.reshape(db, t_new, a),
                           logf_s, cache_k[0].reshape(n_pool_pages, page, a), cache_v[0].reshape(n_pool_pages, page, a),
                           cache_logf[0].transpose(0, 2, 1), page_table, n_heads, dh)
    ext_s = jnp.concatenate([state_pool[0], u_s], axis=1)
    pooled_s = _pool_sample(ext_s.transpose(1, 0, 2), t_new).transpose(1, 0, 2).reshape(1, n_dec, p)
    x1_s, h2_s, idx_s, wt_s = mixer_out(xs3, mod_s, o_att_s.astype(BF16).reshape(1, n_dec, a), pooled_s)

    n_prompt = bsz * seq
    n_tok = n_prompt + n_dec
    h2_all = jnp.concatenate([h2_p.reshape(n_prompt, d), h2_s.reshape(n_dec, d)], axis=0)
    flat_e = jnp.concatenate([idx_p, idx_s], axis=1).T.reshape(-1)
    flat_w = jnp.concatenate([wt_p, wt_s], axis=1).T.reshape(-1)
    n_assign = n_tok * TOP_K
    n_blocks = -(-n_assign // MOE_BLOCK) + n_exp
    n_rows = n_blocks * MOE_BLOCK
    order = jnp.argsort(flat_e, stable=True)
    e_sorted = flat_e[order]
    counts = jnp.bincount(flat_e, length=n_exp)
    padded = (counts + MOE_BLOCK - 1) // MOE_BLOCK * MOE_BLOCK
    pad_end = jnp.cumsum(padded)
    pad_start = pad_end - padded
    start = jnp.cumsum(counts) - counts
    dest = (pad_start[e_sorted] + (jnp.arange(n_assign) - start[e_sorted])).astype(jnp.int32)
    row_tok = jnp.full((n_rows,), n_tok, jnp.int32).at[dest].set((order // TOP_K).astype(jnp.int32))
    row_w = jnp.zeros((n_rows,), F32).at[dest].set(flat_w[order])
    block_e = jnp.minimum(jnp.searchsorted(pad_end, jnp.arange(n_blocks) * MOE_BLOCK, side='right'),
                          n_exp - 1).astype(jnp.int32)
    n_used = (pad_end[-1] // MOE_BLOCK).astype(jnp.int32).reshape(1)
    slot = jnp.zeros((n_assign,), jnp.int32).at[order].set(dest)
    xs = jnp.concatenate([h2_all, jnp.zeros((1, d), BF16)], axis=0)[row_tok]
    ys = _moe(xs, row_w.reshape(n_rows, 1), block_e, n_used, w_gate[0], w_up[0], w_down[0])
    routed = ys[slot.reshape(n_tok, TOP_K)].sum(axis=1)

    y_p = _final(x1_p, h2_p, routed, 0, mod_p[5], wsg, wsu, wsd, gfin)
    y_s = _final(x1_s, h2_s, routed, n_prompt, mod_s[5], wsg, wsu, wsd, gfin)

    heads = lambda z, b, s: z.reshape(1, b, s, n_heads, dh)
    pool_p = u_p[:, seq - (POOL_HALO - 1):][None]
    pool_s = ext_s[:, ext_s.shape[1] - (POOL_HALO - 1):][None]
    return (y_p, y_s.reshape(db, t_new, d), heads(k_p, bsz, seq), heads(v_p, bsz, seq), logf_p[None], pool_p,
            heads(k_s, db, t_new), heads(v_s, db, t_new), logf_s[None], pool_s)
```

```python
import functools

import jax
import jax.numpy as jnp
from jax import lax
from jax.experimental import pallas as pl
from jax.experimental.pallas import tpu as pltpu

F32 = jnp.float32
BF16 = jnp.bfloat16

RMS_EPS = 1e-6
TOP_K = 8
N_GROUPS = 8
TOPK_GROUPS = 4
ROUTED_SCALE = 2.5
POOL_WINDOWS = (2, 4, 8, 16)
POOL_HALO = 16
LANES = 128
MOE_BLOCK = 128
VMEM_LIMIT = 56 * 1024 * 1024

_dot = functools.partial(jnp.dot, preferred_element_type=F32)


def _dot_nt(a, b, precision=None):
    return lax.dot_general(a, b, (((1,), (1,)), ((), ())), preferred_element_type=F32, precision=precision)


def _params(*sem):
    return pltpu.CompilerParams(dimension_semantics=sem, vmem_limit_bytes=VMEM_LIMIT)


def _vmem():
    return pl.BlockSpec(memory_space=pltpu.VMEM)


def _rms(x, g):
    return x * lax.rsqrt(jnp.mean(x * x, axis=-1, keepdims=True) + RMS_EPS) * g


def _rms_mod(x, g, shift, scale):
    return _rms(x, g) * (1.0 + scale) + shift


def _log_sigmoid(x):
    return jnp.minimum(x, 0.0) - jnp.log1p(jnp.exp(-jnp.abs(x)))


def _silu(x):
    return x * jax.nn.sigmoid(x)


def _cumsum_lanes(x):
    n = x.shape[-1]
    lane = lax.broadcasted_iota(jnp.int32, x.shape, x.ndim - 1)
    k = 1
    while k < n:
        x = x + jnp.where(lane >= k, pltpu.roll(x, k, axis=x.ndim - 1), 0.0)
        k *= 2
    return x


def _mod_spec(mod, tm, nt):
    _, r, d = mod.shape
    if r == 1:
        return pl.BlockSpec((1, 1, d), lambda b, i: (b, 0, 0))
    return pl.BlockSpec((1, tm, d), lambda b, i: (b, i, 0))


def _ada_kernel(c_ref, w_ref, b_ref, o_ref):
    a = _silu(c_ref[...]).astype(BF16)
    o_ref[...] = _dot(a, w_ref[...].astype(BF16)) + b_ref[...]


def _ada(c, w, b):
    r, d = c.shape
    n = w.shape[1]
    tn = min(n, 1024)
    return pl.pallas_call(
        _ada_kernel,
        grid=(n // tn,),
        in_specs=[pl.BlockSpec((r, d), lambda j: (0, 0)),
                  pl.BlockSpec((d, tn), lambda j: (0, j)),
                  pl.BlockSpec((1, tn), lambda j: (0, j))],
        out_specs=pl.BlockSpec((r, tn), lambda j: (0, j)),
        out_shape=jax.ShapeDtypeStruct((r, n), F32),
        compiler_params=_params("parallel"),
        name="ada",
    )(c, w, b.reshape(1, n))


def _inproj_kernel(x_ref, g_ref, sh_ref, sc_ref, wq_ref, wk_ref, wv_ref, wu_ref, wf_ref, bf_ref,
                   q_ref, k_ref, v_ref, u_ref, lf_ref, *, qscale):
    h = _rms_mod(x_ref[0], g_ref[...], sh_ref[0], sc_ref[0]).astype(BF16)
    q_ref[0] = (_dot(h, wq_ref[...]) * qscale).astype(BF16)
    k_ref[0] = _dot(h, wk_ref[...])
    v_ref[0] = _dot(h, wv_ref[...])
    u_ref[0] = _dot(h, wu_ref[...])
    lf_ref[0] = _log_sigmoid(_dot(h, wf_ref[...]) + bf_ref[...])


def _inproj(x, g, shift, scale, wq, wk, wv, wu, wf, bf, qscale):
    bsz, s, d = x.shape
    a, p = wq.shape[1], wu.shape[1]
    tm = min(s, 512)
    nt = s // tm
    row = lambda n: pl.BlockSpec((1, tm, n), lambda b, i: (b, i, 0))
    return pl.pallas_call(
        functools.partial(_inproj_kernel, qscale=qscale),
        grid=(bsz, nt),
        in_specs=[row(d), pl.BlockSpec((1, d), lambda b, i: (0, 0)),
                  _mod_spec(shift, tm, nt), _mod_spec(scale, tm, nt),
                  _vmem(), _vmem(), _vmem(), _vmem(), _vmem(), _vmem()],
        out_specs=[row(a), row(a), row(a), row(p), row(LANES)],
        out_shape=[jax.ShapeDtypeStruct((bsz, s, a), BF16),
                   jax.ShapeDtypeStruct((bsz, s, a), F32),
                   jax.ShapeDtypeStruct((bsz, s, a), F32),
                   jax.ShapeDtypeStruct((bsz, s, p), F32),
                   jax.ShapeDtypeStruct((bsz, s, LANES), F32)],
        compiler_params=_params("parallel", "parallel"),
        name="inproj",
    )(x, g, shift, scale, wq, wk, wv, wu, wf, bf)


def _cumsum_kernel(x_ref, o_ref):
    o_ref[0] = _cumsum_lanes(x_ref[0])


def _cumsum_rows(x):
    bsz, h, s = x.shape
    spec = pl.BlockSpec((1, h, s), lambda b: (b, 0, 0))
    return pl.pallas_call(
        _cumsum_kernel, grid=(bsz,), in_specs=[spec], out_specs=spec,
        out_shape=jax.ShapeDtypeStruct(x.shape, F32),
        compiler_params=_params("parallel"), name="logf_cumsum",
    )(x)


def _attn_prompt_kernel(q_ref, k_ref, v_ref, c_ref, o_ref, m_ref, l_ref, acc_ref, *, t, dh):
    qi = pl.program_id(2)
    q = q_ref[0]
    lane = lax.broadcasted_iota(jnp.int32, (1, 2 * dh), 1)
    second = lane >= dh
    zero = jnp.zeros_like(q)
    qh = (jnp.where(second, zero, q), jnp.where(second, q, zero))
    m_ref[...] = jnp.full(m_ref.shape, -jnp.inf, F32)
    l_ref[...] = jnp.zeros(l_ref.shape, F32)
    acc_ref[...] = jnp.zeros(acc_ref.shape, F32)

    def tile(ki, diagonal):
        k0 = pl.multiple_of(ki * t, t)
        kb = k_ref[0, pl.ds(k0, t), :].astype(BF16)
        vb = v_ref[0, pl.ds(k0, t), :].astype(BF16)
        cb = c_ref[0, 0, ki]
        pv, corr = [], []
        for j in range(2):
            s = _dot_nt(qh[j], kb) - cb[j:j + 1, :]
            if diagonal:
                rowi = lax.broadcasted_iota(jnp.int32, (t, t), 0)
                coli = lax.broadcasted_iota(jnp.int32, (t, t), 1)
                s = jnp.where(coli <= rowi, s, -jnp.inf)
            m_old = m_ref[j]
            m_new = jnp.maximum(m_old, jnp.max(s, axis=-1, keepdims=True))
            p = jnp.exp(s - m_new)
            cj = jnp.exp(m_old - m_new)
            l_ref[j] = l_ref[j] * cj + jnp.sum(p, axis=-1, keepdims=True)
            m_ref[j] = m_new
            pv.append(_dot(p.astype(BF16), vb))
            corr.append(cj)
        acc_ref[...] = acc_ref[...] * jnp.where(second, corr[1], corr[0]) + jnp.where(second, pv[1], pv[0])

    def body(ki, carry):
        tile(ki, False)
        return carry

    lax.fori_loop(0, qi, body, 0)
    tile(qi, True)
    o_ref[0] = (acc_ref[...] / jnp.where(second, l_ref[1], l_ref[0])).astype(o_ref.dtype)


def _attn_prompt(q, k, v, cum, dh):
    bsz, s, a = q.shape
    hp = a // (2 * dh)
    t = min(s, 512)
    nt = s // t
    cum = cum.reshape(bsz, hp, 2, nt, t).transpose(0, 1, 3, 2, 4)
    return pl.pallas_call(
        functools.partial(_attn_prompt_kernel, t=t, dh=dh),
        grid=(bsz, hp, nt),
        in_specs=[pl.BlockSpec((1, t, 2 * dh), lambda b, h, i: (b, i, h)),
                  pl.BlockSpec((1, s, 2 * dh), lambda b, h, i: (b, 0, h)),
                  pl.BlockSpec((1, s, 2 * dh), lambda b, h, i: (b, 0, h)),
                  pl.BlockSpec((1, 1, nt, 2, t), lambda b, h, i: (b, h, 0, 0, 0))],
        out_specs=pl.BlockSpec((1, t, 2 * dh), lambda b, h, i: (b, i, h)),
        out_shape=jax.ShapeDtypeStruct((bsz, s, a), BF16),
        scratch_shapes=[pltpu.VMEM((2, t, 1), F32), pltpu.VMEM((2, t, 1), F32),
                        pltpu.VMEM((t, 2 * dh), F32)],
        compiler_params=_params("parallel", "parallel", "parallel"),
        name="attn_prompt",
    )(q, k, v, cum)


def _suffix_and_total(x, stride):
    n = x.shape[1]
    lane = lax.broadcasted_iota(jnp.int32, x.shape, 1)
    sfx = x
    k = stride
    while k < n:
        sfx = sfx + jnp.where(lane + k < n, pltpu.roll(sfx, n - k, axis=1), 0.0)
        k *= 2
    tot = jnp.where(lane < stride, sfx, 0.0)
    k = stride
    while k < n:
        tot = tot + pltpu.roll(tot, k, axis=1)
        k *= 2
    return sfx, tot


def _logf_pages_kernel(x_ref, rin_ref, tot_ref, *, n_heads):
    x = x_ref[...]
    sfx, tot = _suffix_and_total(x, n_heads)
    rin_ref[...] = sfx - x
    tot_ref[...] = tot


def _logf_pages(lf, n_heads):
    n_pool, cols = lf.shape
    pb = 64 if n_pool % 64 == 0 else n_pool
    spec = pl.BlockSpec((pb, cols), lambda i: (i, 0))
    return pl.pallas_call(
        functools.partial(_logf_pages_kernel, n_heads=n_heads),
        grid=(n_pool // pb,), in_specs=[spec], out_specs=[spec, spec],
        out_shape=[jax.ShapeDtypeStruct(lf.shape, F32)] * 2,
        compiler_params=_params("parallel"), name="logf_pages",
    )(lf)


def _decode_kernel(pt_ref, q_ref, kn_ref, vn_ref, lfn_ref, *refs, g_pages, n_pages, n_heads):
    k_refs = refs[:g_pages]
    v_refs = refs[g_pages:2 * g_pages]
    rin_refs = refs[2 * g_pages:3 * g_pages]
    tot_refs = refs[3 * g_pages:4 * g_pages]
    o_ref, m_ref, l_ref, acc_ref, carry_ref, mask_ref = refs[4 * g_pages:]
    seq = pl.program_id(0)
    step = pl.program_id(1)
    q = q_ref[0]
    rows, dh = q.shape
    n_new = rows // n_heads
    cols = carry_ref.shape[1]
    neg = -jnp.inf

    def head_mask(n):
        head_row = lax.broadcasted_iota(jnp.int32, (n_heads, n), 0)
        head_lane = lax.broadcasted_iota(jnp.int32, (n_heads, n), 1) % n_heads
        return jnp.where(head_row == head_lane, 0.0, neg)

    @pl.when(step == 0)
    def _():
        mask_ref[...] = head_mask(cols)
        x = lfn_ref[0]
        sfx, tot = _suffix_and_total(x, n_heads)
        bias = jnp.broadcast_to((sfx - x)[0:1], (n_heads, LANES)) + head_mask(LANES)
        s = _dot_nt(q, kn_ref[0]) + jnp.concatenate([bias] * n_new, axis=0)
        tok = lax.broadcasted_iota(jnp.int32, s.shape, 0) // n_heads
        key = lax.broadcasted_iota(jnp.int32, s.shape, 1) // n_heads
        s = jnp.where(key <= tok, s, neg)
        m = jnp.max(s, axis=-1, keepdims=True)
        p = jnp.exp(s - m)
        m_ref[...] = m
        l_ref[...] = jnp.sum(p, axis=-1, keepdims=True)
        acc_ref[...] = _dot(p.astype(BF16), vn_ref[0])
        tot = jnp.broadcast_to(tot[0:1], (n_heads, LANES))
        carry_ref[...] = jnp.concatenate([tot] * (cols // LANES), axis=1)

    for g in range(g_pages):
        pg = pt_ref[seq * n_pages + n_pages - 1 - (step * g_pages + g)]
        sub = pg % 8
        carry = carry_ref[...]
        r = carry + rin_refs[g][0, pl.ds(sub, 1), :] + mask_ref[...]
        carry_ref[...] = carry + tot_refs[g][0, pl.ds(sub, 1), :]
        kb = k_refs[g][0, 0].reshape(cols, dh).astype(BF16)
        vb = v_refs[g][0, 0].reshape(cols, dh).astype(BF16)
        s = _dot_nt(q, kb) + jnp.concatenate([r] * n_new, axis=0)
        m_old = m_ref[...]
        m_new = jnp.maximum(m_old, jnp.max(s, axis=-1, keepdims=True))
        p = jnp.exp(s - m_new)
        corr = jnp.exp(m_old - m_new)
        l_ref[...] = l_ref[...] * corr + jnp.sum(p, axis=-1, keepdims=True)
        m_ref[...] = m_new
        acc_ref[...] = acc_ref[...] * corr + _dot(p.astype(BF16), vb)

    @pl.when(step == pl.num_programs(1) - 1)
    def _():
        o_ref[0] = acc_ref[...] / l_ref[...]


def _decode_attn(q, k, v, logf, cache_k, cache_v, cache_logf, page_table):
    db, t, a = q.shape
    _, n_pool, page, n_heads, dh = cache_k.shape
    n_pages = page_table.shape[1]
    rows = t * n_heads
    cols = page * n_heads
    assert rows <= LANES and cols % LANES == 0
    g_pages = 8 if n_pages % 8 == 0 else (4 if n_pages % 4 == 0 else 1)

    rin, tot = _logf_pages(cache_logf[0].reshape(n_pool, cols), n_heads)
    pool_pad = -n_pool % 8
    tiles = lambda z: jnp.pad(z, ((0, pool_pad), (0, 0))).reshape((n_pool + pool_pad) // 8, 8, cols)
    rin, tot = tiles(rin), tiles(tot)

    flat = lambda z: jnp.pad(z.astype(BF16).reshape(db, rows, dh), ((0, 0), (0, LANES - rows), (0, 0)))
    lfn = jnp.pad(logf.reshape(db, 1, rows), ((0, 0), (0, 0), (0, LANES - rows)))
    lfn = jnp.broadcast_to(lfn, (db, 8, LANES))

    def page_of(b, s, pt, g):
        return pt[b * n_pages + n_pages - 1 - (s * g_pages + g)]

    per_seq = lambda shape: pl.BlockSpec((1,) + shape, lambda b, s, pt: (b, 0, 0))
    stored = lambda g: pl.BlockSpec((1, 1, page, n_heads, dh), lambda b, s, pt: (0, page_of(b, s, pt, g), 0, 0, 0))
    sums = lambda g: pl.BlockSpec((1, 8, cols), lambda b, s, pt: (page_of(b, s, pt, g) // 8, 0, 0))
    in_specs = [per_seq((rows, dh)), per_seq((LANES, dh)), per_seq((LANES, dh)), per_seq((8, LANES))]
    for spec in (stored, stored, sums, sums):
        in_specs += [spec(g) for g in range(g_pages)]
    grid_spec = pltpu.PrefetchScalarGridSpec(
        num_scalar_prefetch=1, grid=(db, n_pages // g_pages), in_specs=in_specs,
        out_specs=per_seq((rows, dh)),
        scratch_shapes=[pltpu.VMEM((rows, 1), F32), pltpu.VMEM((rows, 1), F32), pltpu.VMEM((rows, dh), F32),
                        pltpu.VMEM((n_heads, cols), F32), pltpu.VMEM((n_heads, cols), F32)])
    out = pl.pallas_call(
        functools.partial(_decode_kernel, g_pages=g_pages, n_pages=n_pages, n_heads=n_heads),
        grid_spec=grid_spec,
        out_shape=jax.ShapeDtypeStruct((db, rows, dh), F32),
        compiler_params=_params("parallel", "arbitrary"),
        name="attn_decode",
    )(page_table.reshape(-1), q.reshape(db, rows, dh), flat(k), flat(v), lfn,
      *([cache_k] * g_pages), *([cache_v] * g_pages), *([rin] * g_pages), *([tot] * g_pages))
    return out.reshape(db, t, a)


def _pool_prompt_kernel(u_ref, halo_ref, o_ref, *, tm, pg):
    i = pl.program_id(1)
    u = u_ref[0]
    halo = jnp.where(i > 0, halo_ref[0], 0.0)
    ext = jnp.concatenate([halo, u], axis=0)
    pos = i * tm + lax.broadcasted_iota(jnp.int32, (tm, 1), 0)
    outs = []
    for g, w in enumerate(POOL_WINDOWS):
        r = ext[:, g * pg:(g + 1) * pg]
        k = 1
        while k < w:
            r = r + pltpu.roll(r, k, axis=0)
            k *= 2
        count = jnp.minimum(w, pos + 1).astype(F32)
        outs.append(r[POOL_HALO:] / count - u[:, g * pg:(g + 1) * pg])
    o_ref[0] = jnp.concatenate(outs, axis=-1).astype(o_ref.dtype)


def _pool_prompt(u):
    bsz, s, p = u.shape
    tm = min(s, 512)
    hb = tm // POOL_HALO
    return pl.pallas_call(
        functools.partial(_pool_prompt_kernel, tm=tm, pg=p // len(POOL_WINDOWS)),
        grid=(bsz, s // tm),
        in_specs=[pl.BlockSpec((1, tm, p), lambda b, i: (b, i, 0)),
                  pl.BlockSpec((1, POOL_HALO, p), lambda b, i: (b, jnp.maximum(i * hb - 1, 0), 0))],
        out_specs=pl.BlockSpec((1, tm, p), lambda b, i: (b, i, 0)),
        out_shape=jax.ShapeDtypeStruct(u.shape, BF16),
        compiler_params=_params("parallel", "parallel"),
        name="pool_prompt",
    )(u, u)


def _pool_sample_kernel(e_ref, o_ref, *, n_new, n_prev, pg):
    for t in range(n_new):
        e = n_prev + t
        outs = []
        for g, w in enumerate(POOL_WINDOWS):
            cols = slice(g * pg, (g + 1) * pg)
            lo = max(0, e - w + 1)
            acc = e_ref[lo, :, cols]
            for j in range(lo + 1, e + 1):
                acc = acc + e_ref[j, :, cols]
            outs.append(acc / float(min(w, e + 1)) - e_ref[e, :, cols])
        o_ref[t] = jnp.concatenate(outs, axis=-1).astype(o_ref.dtype)


def _pool_sample(ext_t, n_new):
    n_all, db, p = ext_t.shape
    return pl.pallas_call(
        functools.partial(_pool_sample_kernel, n_new=n_new, n_prev=n_all - n_new, pg=p // len(POOL_WINDOWS)),
        in_specs=[_vmem()], out_specs=_vmem(),
        out_shape=jax.ShapeDtypeStruct((n_new, db, p), BF16),
        compiler_params=pltpu.CompilerParams(vmem_limit_bytes=VMEM_LIMIT),
        name="pool_sample",
    )(ext_t)


def _merge_kernel(x_ref, g_ref, sh_ref, sc_ref, gt_ref, oa_ref, pl_ref, wpool_ref, ps_ref,
                  wpa_ref, wpb_ref, wga_ref, wgb_ref, wo_ref, o_ref):
    x = x_ref[0]
    h = _rms_mod(x, g_ref[...], sh_ref[0], sc_ref[0]).astype(BF16)
    pooled = pl_ref[0]
    n_groups, pg, _ = wpool_ref.shape
    mixed = jnp.concatenate(
        [_dot(pooled[:, g * pg:(g + 1) * pg], wpool_ref[g]) for g in range(n_groups)], axis=-1)
    o_pool = (mixed * ps_ref[...]).astype(BF16)
    y = jax.nn.sigmoid(_dot(h, wga_ref[...])) * _dot(oa_ref[0], wpa_ref[...])
    y = y + jax.nn.sigmoid(_dot(h, wgb_ref[...])) * _dot(o_pool, wpb_ref[...])
    o_ref[0] = x + gt_ref[0] * _dot(y.astype(BF16), wo_ref[...])


def _merge(x, g, shift, scale, gate, o_att, pooled, w_pool, pool_scale, w_pa, w_pb, w_ga, w_gb, w_o):
    bsz, s, d = x.shape
    tm = min(s, 256)
    nt = s // tm
    row = lambda n: pl.BlockSpec((1, tm, n), lambda b, i: (b, i, 0))
    return pl.pallas_call(
        _merge_kernel,
        grid=(bsz, nt),
        in_specs=[row(d), pl.BlockSpec((1, d), lambda b, i: (0, 0)),
                  _mod_spec(shift, tm, nt), _mod_spec(scale, tm, nt), _mod_spec(gate, tm, nt),
                  row(o_att.shape[2]), row(pooled.shape[2]),
                  _vmem(), _vmem(), _vmem(), _vmem(), _vmem(), _vmem(), _vmem()],
        out_specs=row(d),
        out_shape=jax.ShapeDtypeStruct(x.shape, F32),
        compiler_params=_params("parallel", "parallel"),
        name="merge",
    )(x, g, shift, scale, gate, o_att, pooled, w_pool, pool_scale, w_pa, w_pb, w_ga, w_gb, w_o)


def _route_kernel(x_ref, g_ref, sh_ref, sc_ref, wr_ref, rb_ref, h_ref, idx_ref, w_ref):
    h = _rms_mod(x_ref[0], g_ref[...], sh_ref[0], sc_ref[0])
    h_ref[0] = h.astype(BF16)
    n_exp = wr_ref.shape[0]
    tm = h.shape[0]
    per = n_exp // N_GROUPS
    scores = jax.nn.sigmoid(_dot_nt(wr_ref[...], h, precision=lax.Precision.HIGHEST))
    sel = scores + rb_ref[...]
    neg = -jnp.inf
    groups = [sel[g * per:(g + 1) * per] for g in range(N_GROUPS)]
    eidx = lax.broadcasted_iota(jnp.int32, (per, tm), 0)
    gscore = []
    for grp in groups:
        m1 = jnp.max(grp, axis=0, keepdims=True)
        first = jnp.min(jnp.where(grp == m1, eidx, per), axis=0, keepdims=True)
        gscore.append(m1 + jnp.max(jnp.where(eidx == first, neg, grp), axis=0, keepdims=True))
    cand = []
    for g in range(N_GROUPS):
        rank = jnp.zeros((1, tm), jnp.int32)
        for o in range(N_GROUPS):
            if o != g:
                ahead = gscore[o] >= gscore[g] if o < g else gscore[o] > gscore[g]
                rank = rank + ahead.astype(jnp.int32)
        cand.append(jnp.where(rank < TOPK_GROUPS, groups[g], neg))
    cand = jnp.concatenate(cand, axis=0)
    row = lax.broadcasted_iota(jnp.int32, (n_exp, tm), 0)
    idxs, wts = [], []
    for _ in range(TOP_K):
        m = jnp.max(cand, axis=0, keepdims=True)
        pick = jnp.min(jnp.where(cand == m, row, n_exp), axis=0, keepdims=True)
        hit = row == pick
        idxs.append(pick)
        wts.append(jnp.sum(jnp.where(hit, scores, 0.0), axis=0, keepdims=True))
        cand = jnp.where(hit, neg, cand)
    wts = jnp.concatenate(wts, axis=0)
    idx_ref[...] = jnp.concatenate(idxs, axis=0)
    w_ref[...] = wts / jnp.sum(wts, axis=0, keepdims=True) * ROUTED_SCALE


def _route(x, g, shift, scale, w_router_t, router_bias):
    bsz, s, d = x.shape
    tm = min(s, 256)
    nt = s // tm
    col = pl.BlockSpec((TOP_K, tm), lambda b, i: (0, b * nt + i))
    return pl.pallas_call(
        _route_kernel,
        grid=(bsz, nt),
        in_specs=[pl.BlockSpec((1, tm, d), lambda b, i: (b, i, 0)), pl.BlockSpec((1, d), lambda b, i: (0, 0)),
                  _mod_spec(shift, tm, nt), _mod_spec(scale, tm, nt), _vmem(), _vmem()],
        out_specs=[pl.BlockSpec((1, tm, d), lambda b, i: (b, i, 0)), col, col],
        out_shape=[jax.ShapeDtypeStruct(x.shape, BF16),
                   jax.ShapeDtypeStruct((TOP_K, bsz * s), jnp.int32),
                   jax.ShapeDtypeStruct((TOP_K, bsz * s), F32)],
        compiler_params=_params("parallel", "parallel"),
        name="route",
    )(x, g, shift, scale, w_router_t, router_bias)


def _moe_kernel(be_ref, nu_ref, lead_ref, nxt_ref, par_ref, xs_ref, wt_ref, wg_hbm, wu_hbm, wd_hbm, o_ref,
                wg_f, wu_f, wd_f, wg_b, wu_b, wd_b, sems):
    i = pl.program_id(0)
    used = i < nu_ref[0]

    def fetch(expert, slot):
        return [pltpu.make_async_copy(src.at[expert], dst.at[slot], sems.at[n, slot])
                for n, (src, dst) in enumerate(((wg_hbm, wg_f), (wu_hbm, wu_f), (wd_hbm, wd_f)))]

    @pl.when(used & (i == 0))
    def _():
        for c in fetch(be_ref[0], 0):
            c.start()

    @pl.when(used & (lead_ref[i] == 1))
    def _():
        slot = par_ref[i]
        nxt = nxt_ref[i]

        @pl.when(nxt >= 0)
        def _():
            for c in fetch(nxt, 1 - slot):
                c.start()

        for c in fetch(be_ref[i], slot):
            c.wait()
        wg_b[...] = wg_f[slot].astype(BF16)
        wu_b[...] = wu_f[slot].astype(BF16)
        wd_b[...] = wd_f[slot].astype(BF16)

    @pl.when(used)
    def _():
        xb = xs_ref[...]
        hid = (_silu(_dot(xb, wg_b[...])) * _dot(xb, wu_b[...])).astype(BF16)
        o_ref[...] = (_dot(hid, wd_b[...]) * wt_ref[...]).astype(o_ref.dtype)

    @pl.when(jnp.logical_not(used))
    def _():
        o_ref[...] = jnp.zeros(o_ref.shape, o_ref.dtype)


def _moe(xs, row_w, block_e, n_used, w_gate, w_up, w_down):
    n_rows, d = xs.shape
    n_exp, _, f = w_gate.shape
    n_blocks = n_rows // MOE_BLOCK
    blk = jnp.arange(n_blocks, dtype=jnp.int32)
    used = blk < n_used[0]
    lead = used & jnp.concatenate([jnp.ones((1,), bool), block_e[1:] != block_e[:-1]])
    parity = (jnp.cumsum(lead) - 1) % 2
    lead_at = jnp.where(lead, blk, n_blocks)
    nxt_blk = lax.cummin(jnp.concatenate([lead_at[1:], jnp.full((1,), n_blocks, jnp.int32)]), reverse=True)
    nxt = jnp.where(nxt_blk < n_blocks, block_e[jnp.minimum(nxt_blk, n_blocks - 1)], -1)
    row = lambda n: pl.BlockSpec((MOE_BLOCK, n), lambda i, *_: (i, 0))
    hbm = pl.BlockSpec(memory_space=pl.ANY)
    grid_spec = pltpu.PrefetchScalarGridSpec(
        num_scalar_prefetch=5, grid=(n_blocks,),
        in_specs=[row(d), row(1), hbm, hbm, hbm],
        out_specs=row(d),
        scratch_shapes=[pltpu.VMEM((2, d, f), F32), pltpu.VMEM((2, d, f), F32), pltpu.VMEM((2, f, d), F32),
                        pltpu.VMEM((d, f), BF16), pltpu.VMEM((d, f), BF16), pltpu.VMEM((f, d), BF16),
                        pltpu.SemaphoreType.DMA((3, 2))])
    return pl.pallas_call(
        _moe_kernel, grid_spec=grid_spec,
        out_shape=jax.ShapeDtypeStruct((n_rows, d), BF16),
        compiler_params=_params("arbitrary"),
        name="moe",
    )(block_e, n_used, lead.astype(jnp.int32), nxt.astype(jnp.int32), parity.astype(jnp.int32),
      xs, row_w, w_gate, w_up, w_down)


def _final_kernel(x_ref, h_ref, r_ref, gt_ref, wg_ref, wu_ref, wd_ref, gf_ref, o_ref):
    hb = h_ref[0]
    hid = (_silu(_dot(hb, wg_ref[...])) * _dot(hb, wu_ref[...])).astype(BF16)
    moe = r_ref[...] + _dot(hid, wd_ref[...])
    o_ref[0] = _rms(x_ref[0] + gt_ref[0] * moe, gf_ref[...])


def _final(x, h, routed, row0, gate, w_sg, w_su, w_sd, g_final):
    bsz, s, d = x.shape
    tm = min(s, 256)
    nt = s // tm
    assert row0 % tm == 0
    rb0 = row0 // tm
    row = pl.BlockSpec((1, tm, d), lambda b, i: (b, i, 0))
    return pl.pallas_call(
        _final_kernel,
        grid=(bsz, nt),
        in_specs=[row, row, pl.BlockSpec((tm, d), lambda b, i: (rb0 + b * nt + i, 0)),
                  _mod_spec(gate, tm, nt), _vmem(), _vmem(), _vmem(), pl.BlockSpec((1, d), lambda b, i: (0, 0))],
        out_specs=row,
        out_shape=jax.ShapeDtypeStruct(x.shape, F32),
        compiler_params=_params("parallel", "parallel"),
        name="final",
    )(x, h, routed, gate, w_sg, w_su, w_sd, g_final)


def kernel(x_prompt, x_sample, cache_k, cache_v, cache_logf, state_pool, page_table, c_prompt, c_sample, w_ada, b_ada, g_mix, w_in, b_f, w_pool, pool_scale, w_pa, w_pb, w_o, g_ffn, w_router, router_bias, w_gate, w_up, w_down, w_sh_gate, w_sh_up, w_sh_down, g_final):
    depth = w_ada.shape[0]
    assert depth == 1
    bsz, seq, d = x_prompt.shape
    db, t_new, _ = x_sample.shape
    _, n_pool_pages, page, n_heads, dh = cache_k.shape
    a = n_heads * dh
    p = state_pool.shape[3]
    n_exp = w_router.shape[2]
    n_dec = db * t_new
    qscale = dh ** -0.5

    w = w_in[0]
    cuts = [a, 2 * a, 3 * a, 3 * a + n_heads, 3 * a + n_heads + p, 3 * a + n_heads + p + d]
    wq, wk, wv, wf, wu, wga, wgb = [z.astype(BF16) for z in jnp.split(w, cuts, axis=1)]
    wf = jnp.pad(wf, ((0, 0), (0, LANES - n_heads)))
    bf = jnp.pad(b_f[0], (0, LANES - n_heads)).reshape(1, LANES)
    wpool, wpa, wpb, wo = w_pool[0].astype(BF16), w_pa[0].astype(BF16), w_pb[0].astype(BF16), w_o[0].astype(BF16)
    wsg, wsu, wsd = w_sh_gate[0].astype(BF16), w_sh_up[0].astype(BF16), w_sh_down[0].astype(BF16)
    wr_t = w_router[0].T
    rbias = router_bias[0].reshape(n_exp, 1)
    gmix, gffn, gfin = g_mix[0].reshape(1, d), g_ffn[0].reshape(1, d), g_final.reshape(1, d)
    pscale = pool_scale[0].reshape(1, p)

    c_all = jnp.concatenate([c_prompt, c_sample], axis=0)
    r_pad = -c_all.shape[0] % 16
    mod = _ada(jnp.pad(c_all, ((0, r_pad), (0, 0))), w_ada[0], b_ada[0])
    mod_p = [m.reshape(bsz, 1, d) for m in jnp.split(mod[:bsz], 6, axis=-1)]
    mod_s = [jnp.repeat(m, t_new, axis=0).reshape(1, n_dec, d) for m in jnp.split(mod[bsz:bsz + db], 6, axis=-1)]

    def mixer_in(x, m):
        return _inproj(x, gmix, m[0], m[1], wq, wk, wv, wu, wf, bf, qscale)

    def mixer_out(x, m, o_att, pooled):
        x1 = _merge(x, gmix, m[0], m[1], m[2], o_att, pooled, wpool, pscale, wpa, wpb, wga, wgb, wo)
        return (x1,) + tuple(_route(x1, gffn, m[3], m[4], wr_t, rbias))

    q_p, k_p, v_p, u_p, lf_p = mixer_in(x_prompt, mod_p)
    logf_p = lf_p[:, :, :n_heads]
    cum_p = _cumsum_rows(logf_p.transpose(0, 2, 1))
    o_att_p = _attn_prompt(q_p, k_p, v_p, cum_p, dh)
    x1_p, h2_p, idx_p, wt_p = mixer_out(x_prompt, mod_p, o_att_p, _pool_prompt(u_p))

    xs3 = x_sample.reshape(1, n_dec, d)
    q_s, k_s, v_s, u_s, lf_s = mixer_in(xs3, mod_s)
    logf_s = lf_s[0, :, :n_heads].reshape(db, t_new, n_heads)
    u_s = u_s.reshape(db, t_new, p)
    o_att_s = _decode_attn(q_s.reshape(db, t_new, a), k_s.reshape(db, t_new, a), v_s.reshape(db, t_new, a),
                           logf_s, cache_k, cache_v, cache_logf, page_table)
    ext_s = jnp.concatenate([state_pool[0], u_s], axis=1)
    pooled_s = _pool_sample(ext_s.transpose(1, 0, 2), t_new).transpose(1, 0, 2).reshape(1, n_dec, p)
    x1_s, h2_s, idx_s, wt_s = mixer_out(xs3, mod_s, o_att_s.astype(BF16).reshape(1, n_dec, a), pooled_s)

    n_prompt = bsz * seq
    n_tok = n_prompt + n_dec
    h2_all = jnp.concatenate([h2_p.reshape(n_prompt, d), h2_s.reshape(n_dec, d)], axis=0)
    flat_e = jnp.concatenate([idx_p, idx_s], axis=1).T.reshape(-1)
    flat_w = jnp.concatenate([wt_p, wt_s], axis=1).T.reshape(-1)
    n_assign = n_tok * TOP_K
    n_blocks = -(-n_assign // MOE_BLOCK) + n_exp
    n_rows = n_blocks * MOE_BLOCK
    order = jnp.argsort(flat_e, stable=True).astype(jnp.int32)
    rank = jnp.argsort(order).astype(jnp.int32)
    e_sorted = flat_e[order]
    experts = jnp.arange(n_exp, dtype=jnp.int32)
    start = jnp.searchsorted(e_sorted, experts, side='left').astype(jnp.int32)
    counts = jnp.searchsorted(e_sorted, experts, side='right').astype(jnp.int32) - start
    padded = (counts + MOE_BLOCK - 1) // MOE_BLOCK * MOE_BLOCK
    pad_end = jnp.cumsum(padded)
    pad_start = pad_end - padded
    block_e = jnp.minimum(jnp.searchsorted(pad_end, jnp.arange(n_blocks) * MOE_BLOCK, side='right'),
                          n_exp - 1).astype(jnp.int32)
    n_used = (pad_end[-1] // MOE_BLOCK).astype(jnp.int32).reshape(1)
    slot = pad_start[flat_e] + rank - start[flat_e]
    row_e = jnp.repeat(block_e, MOE_BLOCK)
    within = jnp.arange(n_rows, dtype=jnp.int32) - pad_start[row_e]
    live = (within < counts[row_e]) & (jnp.arange(n_rows) < pad_end[-1])
    src = order[jnp.clip(start[row_e] + within, 0, n_assign - 1)]
    row_tok = jnp.where(live, src // TOP_K, n_tok)
    row_w = jnp.where(live, flat_w[src], 0.0)
    xs = jnp.concatenate([h2_all, jnp.zeros((1, d), BF16)], axis=0)[row_tok]
    ys = _moe(xs, row_w.reshape(n_rows, 1), block_e, n_used, w_gate[0], w_up[0], w_down[0])
    routed = ys[slot.reshape(n_tok, TOP_K)].astype(F32).sum(axis=1)

    y_p = _final(x1_p, h2_p, routed, 0, mod_p[5], wsg, wsu, wsd, gfin)
    y_s = _final(x1_s, h2_s, routed, n_prompt, mod_s[5], wsg, wsu, wsd, gfin)

    heads = lambda z, b, s: z.reshape(1, b, s, n_heads, dh)
    pool_p = u_p[:, seq - (POOL_HALO - 1):][None]
    pool_s = ext_s[:, ext_s.shape[1] - (POOL_HALO - 1):][None]
    return (y_p, y_s.reshape(db, t_new, d), heads(k_p, bsz, seq), heads(v_p, bsz, seq), logf_p[None], pool_p,
            heads(k_s, db, t_new), heads(v_s, db, t_new), logf_s[None], pool_s)
```

```python
import functools

import jax
import jax.numpy as jnp
from jax import lax
from jax.experimental import pallas as pl
from jax.experimental.pallas import tpu as pltpu

F32 = jnp.float32
BF16 = jnp.bfloat16

RMS_EPS = 1e-6
TOP_K = 8
N_GROUPS = 8
TOPK_GROUPS = 4
ROUTED_SCALE = 2.5
POOL_WINDOWS = (2, 4, 8, 16)
POOL_HALO = 16
LANES = 128
MOE_BLOCK = 128
VMEM_LIMIT = 56 * 1024 * 1024

_dot = functools.partial(jnp.dot, preferred_element_type=F32)


def _dot_nt(a, b, precision=None):
    return lax.dot_general(a, b, (((1,), (1,)), ((), ())), preferred_element_type=F32, precision=precision)


def _params(*sem):
    return pltpu.CompilerParams(dimension_semantics=sem, vmem_limit_bytes=VMEM_LIMIT)


def _vmem():
    return pl.BlockSpec(memory_space=pltpu.VMEM)


def _rms(x, g):
    return x * lax.rsqrt(jnp.mean(x * x, axis=-1, keepdims=True) + RMS_EPS) * g


def _rms_mod(x, g, shift, scale):
    return _rms(x, g) * (1.0 + scale) + shift


def _log_sigmoid(x):
    return jnp.minimum(x, 0.0) - jnp.log1p(jnp.exp(-jnp.abs(x)))


def _silu(x):
    return x * jax.nn.sigmoid(x)


def _cumsum_lanes(x):
    n = x.shape[-1]
    lane = lax.broadcasted_iota(jnp.int32, x.shape, x.ndim - 1)
    k = 1
    while k < n:
        x = x + jnp.where(lane >= k, pltpu.roll(x, k, axis=x.ndim - 1), 0.0)
        k *= 2
    return x


def _mod_spec(mod, tm, nt):
    _, r, d = mod.shape
    if r == 1:
        return pl.BlockSpec((1, 1, d), lambda b, i: (b, 0, 0))
    return pl.BlockSpec((1, tm, d), lambda b, i: (b, i, 0))


def _ada_kernel(c_ref, w_ref, b_ref, o_ref):
    a = _silu(c_ref[...]).astype(BF16)
    o_ref[...] = _dot(a, w_ref[...].astype(BF16)) + b_ref[...]


def _ada(c, w, b):
    r, d = c.shape
    n = w.shape[1]
    tn = min(n, 1024)
    return pl.pallas_call(
        _ada_kernel,
        grid=(n // tn,),
        in_specs=[pl.BlockSpec((r, d), lambda j: (0, 0)),
                  pl.BlockSpec((d, tn), lambda j: (0, j)),
                  pl.BlockSpec((1, tn), lambda j: (0, j))],
        out_specs=pl.BlockSpec((r, tn), lambda j: (0, j)),
        out_shape=jax.ShapeDtypeStruct((r, n), F32),
        compiler_params=_params("parallel"),
        name="ada",
    )(c, w, b.reshape(1, n))


def _inproj_kernel(x_ref, g_ref, sh_ref, sc_ref, wq_ref, wk_ref, wv_ref, wu_ref, wf_ref, bf_ref,
                   q_ref, k_ref, v_ref, u_ref, lf_ref, *, qscale):
    h = _rms_mod(x_ref[0], g_ref[...], sh_ref[0], sc_ref[0]).astype(BF16)
    q_ref[0] = (_dot(h, wq_ref[...]) * qscale).astype(BF16)
    k_ref[0] = _dot(h, wk_ref[...])
    v_ref[0] = _dot(h, wv_ref[...])
    u_ref[0] = _dot(h, wu_ref[...])
    lf_ref[0] = _log_sigmoid(_dot(h, wf_ref[...]) + bf_ref[...])


def _inproj(x, g, shift, scale, wq, wk, wv, wu, wf, bf, qscale):
    bsz, s, d = x.shape
    a, p = wq.shape[1], wu.shape[1]
    tm = min(s, 512)
    nt = s // tm
    row = lambda n: pl.BlockSpec((1, tm, n), lambda b, i: (b, i, 0))
    return pl.pallas_call(
        functools.partial(_inproj_kernel, qscale=qscale),
        grid=(bsz, nt),
        in_specs=[row(d), pl.BlockSpec((1, d), lambda b, i: (0, 0)),
                  _mod_spec(shift, tm, nt), _mod_spec(scale, tm, nt),
                  _vmem(), _vmem(), _vmem(), _vmem(), _vmem(), _vmem()],
        out_specs=[row(a), row(a), row(a), row(p), row(LANES)],
        out_shape=[jax.ShapeDtypeStruct((bsz, s, a), BF16),
                   jax.ShapeDtypeStruct((bsz, s, a), F32),
                   jax.ShapeDtypeStruct((bsz, s, a), F32),
                   jax.ShapeDtypeStruct((bsz, s, p), F32),
                   jax.ShapeDtypeStruct((bsz, s, LANES), F32)],
        compiler_params=_params("parallel", "parallel"),
        name="inproj",
    )(x, g, shift, scale, wq, wk, wv, wu, wf, bf)


def _cumsum_kernel(x_ref, o_ref):
    o_ref[0] = _cumsum_lanes(x_ref[0])


def _cumsum_rows(x):
    bsz, h, s = x.shape
    spec = pl.BlockSpec((1, h, s), lambda b: (b, 0, 0))
    return pl.pallas_call(
        _cumsum_kernel, grid=(bsz,), in_specs=[spec], out_specs=spec,
        out_shape=jax.ShapeDtypeStruct(x.shape, F32),
        compiler_params=_params("parallel"), name="logf_cumsum",
    )(x)


def _attn_prompt_kernel(q_ref, k_ref, v_ref, c_ref, o_ref, m_ref, l_ref, acc_ref, *, t, dh):
    qi = pl.program_id(2)
    q = q_ref[0]
    lane = lax.broadcasted_iota(jnp.int32, (1, 2 * dh), 1)
    second = lane >= dh
    zero = jnp.zeros_like(q)
    qh = (jnp.where(second, zero, q), jnp.where(second, q, zero))
    m_ref[...] = jnp.full(m_ref.shape, -jnp.inf, F32)
    l_ref[...] = jnp.zeros(l_ref.shape, F32)
    acc_ref[...] = jnp.zeros(acc_ref.shape, F32)

    def tile(ki, diagonal):
        k0 = pl.multiple_of(ki * t, t)
        kb = k_ref[0, pl.ds(k0, t), :].astype(BF16)
        vb = v_ref[0, pl.ds(k0, t), :].astype(BF16)
        cb = c_ref[0, 0, ki]
        pv, corr = [], []
        for j in range(2):
            s = _dot_nt(qh[j], kb) - cb[j:j + 1, :]
            if diagonal:
                rowi = lax.broadcasted_iota(jnp.int32, (t, t), 0)
                coli = lax.broadcasted_iota(jnp.int32, (t, t), 1)
                s = jnp.where(coli <= rowi, s, -jnp.inf)
            m_old = m_ref[j]
            m_new = jnp.maximum(m_old, jnp.max(s, axis=-1, keepdims=True))
            p = jnp.exp(s - m_new)
            cj = jnp.exp(m_old - m_new)
            l_ref[j] = l_ref[j] * cj + jnp.sum(p, axis=-1, keepdims=True)
            m_ref[j] = m_new
            pv.append(_dot(p.astype(BF16), vb))
            corr.append(cj)
        acc_ref[...] = acc_ref[...] * jnp.where(second, corr[1], corr[0]) + jnp.where(second, pv[1], pv[0])

    def body(ki, carry):
        tile(ki, False)
        return carry

    lax.fori_loop(0, qi, body, 0)
    tile(qi, True)
    o_ref[0] = (acc_ref[...] / jnp.where(second, l_ref[1], l_ref[0])).astype(o_ref.dtype)


def _attn_prompt(q, k, v, cum, dh):
    bsz, s, a = q.shape
    hp = a // (2 * dh)
    t = min(s, 512)
    nt = s // t
    cum = cum.reshape(bsz, hp, 2, nt, t).transpose(0, 1, 3, 2, 4)
    return pl.pallas_call(
        functools.partial(_attn_prompt_kernel, t=t, dh=dh),
        grid=(bsz, hp, nt),
        in_specs=[pl.BlockSpec((1, t, 2 * dh), lambda b, h, i: (b, i, h)),
                  pl.BlockSpec((1, s, 2 * dh), lambda b, h, i: (b, 0, h)),
                  pl.BlockSpec((1, s, 2 * dh), lambda b, h, i: (b, 0, h)),
                  pl.BlockSpec((1, 1, nt, 2, t), lambda b, h, i: (b, h, 0, 0, 0))],
        out_specs=pl.BlockSpec((1, t, 2 * dh), lambda b, h, i: (b, i, h)),
        out_shape=jax.ShapeDtypeStruct((bsz, s, a), BF16),
        scratch_shapes=[pltpu.VMEM((2, t, 1), F32), pltpu.VMEM((2, t, 1), F32),
                        pltpu.VMEM((t, 2 * dh), F32)],
        compiler_params=_params("parallel", "parallel", "parallel"),
        name="attn_prompt",
    )(q, k, v, cum)


def _logf_pages_kernel(x_ref, rin_ref, tot_ref):
    x = x_ref[...]
    n = x.shape[1]
    lane = lax.broadcasted_iota(jnp.int32, x.shape, 1)
    sfx = x
    k = 1
    while k < n:
        sfx = sfx + jnp.where(lane + k < n, pltpu.roll(sfx, n - k, axis=1), 0.0)
        k *= 2
    rin_ref[...] = sfx - x
    tot_ref[...] = jnp.broadcast_to(sfx[:, 0:1], x.shape)


def _logf_pages(lf_t):
    n_pool, n_heads, page = lf_t.shape
    pb = 64 if n_pool % 64 == 0 else n_pool
    spec = pl.BlockSpec((pb * n_heads, page), lambda i: (i, 0))
    rin, tot = pl.pallas_call(
        _logf_pages_kernel,
        grid=(n_pool // pb,), in_specs=[spec], out_specs=[spec, spec],
        out_shape=[jax.ShapeDtypeStruct((n_pool * n_heads, page), F32)] * 2,
        compiler_params=_params("parallel"), name="logf_pages",
    )(lf_t.reshape(n_pool * n_heads, page))
    return rin.reshape(lf_t.shape), tot.reshape(lf_t.shape)


def _decode_kernel(pt_ref, qb_ref, kn_ref, vn_ref, lfn_ref, *refs, g_pages, n_new, n_heads, dh):
    k_refs = refs[:g_pages]
    v_refs = refs[g_pages:2 * g_pages]
    rin_refs = refs[2 * g_pages:3 * g_pages]
    tot_refs = refs[3 * g_pages:4 * g_pages]
    o_ref, m_ref, l_ref, acc_ref, carry_ref = refs[4 * g_pages:]
    step = pl.program_id(1)
    qb = qb_ref[0]

    @pl.when(step == 0)
    def _():
        c = _cumsum_lanes(lfn_ref[0])
        tot = c[:, LANES - 1:LANES]
        rnew = (tot - c)[:, :16]
        s = _dot_nt(qb, kn_ref[0]) + jnp.concatenate([rnew] * n_new, axis=0)
        tok = lax.broadcasted_iota(jnp.int32, s.shape, 0) // n_heads
        key = lax.broadcasted_iota(jnp.int32, s.shape, 1)
        s = jnp.where(key <= tok, s, -jnp.inf)
        m = jnp.max(s, axis=-1, keepdims=True)
        p = jnp.exp(s - m)
        m_ref[...] = m
        l_ref[...] = jnp.sum(p, axis=-1, keepdims=True)
        acc_ref[...] = _dot(p.astype(BF16), vn_ref[0])
        carry_ref[...] = jnp.broadcast_to(tot, carry_ref.shape)

    carry = carry_ref[...]
    scores = []
    for g in range(g_pages):
        r = carry + rin_refs[g][0]
        carry = carry + tot_refs[g][0]
        kb = k_refs[g][0].astype(BF16)
        scores.append(_dot(qb, kb) + jnp.concatenate([r] * n_new, axis=0))
    carry_ref[...] = carry
    s = jnp.concatenate(scores, axis=1)
    m_old = m_ref[...]
    m_new = jnp.maximum(m_old, jnp.max(s, axis=-1, keepdims=True))
    p = jnp.exp(s - m_new)
    corr = jnp.exp(m_old - m_new)
    l_ref[...] = l_ref[...] * corr + jnp.sum(p, axis=-1, keepdims=True)
    m_ref[...] = m_new
    pb = p.astype(BF16)
    page = s.shape[1] // g_pages
    pv = _dot_nt(pb[:, :page], v_refs[0][0].astype(BF16))
    for g in range(1, g_pages):
        pv = pv + _dot_nt(pb[:, g * page:(g + 1) * page], v_refs[g][0].astype(BF16))
    acc_ref[...] = acc_ref[...] * corr + pv

    @pl.when(step == pl.num_programs(1) - 1)
    def _():
        o = acc_ref[...] / l_ref[...]
        head = lax.broadcasted_iota(jnp.int32, o.shape, 0) % n_heads
        lane_head = lax.broadcasted_iota(jnp.int32, o.shape, 1) // dh
        o = jnp.where(head == lane_head, o, 0.0)
        o_ref[0] = jnp.sum(o.reshape(n_new, n_heads, o.shape[1]), axis=1)


def _decode_attn(q, k, v, logf, cache_kt, cache_vt, cache_logf_t, page_table, n_heads, dh):
    db, t, a = q.shape
    n_pages = page_table.shape[1]
    page = cache_kt.shape[2]
    assert page == LANES and t <= 16
    g_pages = 8 if n_pages % 8 == 0 else (4 if n_pages % 4 == 0 else 1)
    rows = t * n_heads
    rin, tot = _logf_pages(cache_logf_t)
    eye = jnp.eye(n_heads, dtype=BF16)
    qb = (q.reshape(db, t, 1, n_heads, dh) * eye[None, None, :, :, None]).reshape(db, rows, a)
    pad = lambda z: jnp.pad(z.astype(BF16), ((0, 0), (0, 16 - t), (0, 0)))
    lfn = jnp.pad(logf.transpose(0, 2, 1), ((0, 0), (0, 0), (0, LANES - t)))

    def page_idx(g):
        return lambda b, s, pt: (pt[b * n_pages + n_pages - 1 - (s * g_pages + g)], 0, 0)

    seq = lambda shape: pl.BlockSpec((1,) + shape, lambda b, s, pt: (b, 0, 0))
    in_specs = [seq((rows, a)), seq((16, a)), seq((16, a)), seq((n_heads, LANES))]
    for shape in ((1, a, page), (1, a, page), (1, n_heads, page), (1, n_heads, page)):
        in_specs += [pl.BlockSpec(shape, page_idx(g)) for g in range(g_pages)]
    grid_spec = pltpu.PrefetchScalarGridSpec(
        num_scalar_prefetch=1, grid=(db, n_pages // g_pages), in_specs=in_specs,
        out_specs=seq((t, a)),
        scratch_shapes=[pltpu.VMEM((rows, 1), F32), pltpu.VMEM((rows, 1), F32),
                        pltpu.VMEM((rows, a), F32), pltpu.VMEM((n_heads, page), F32)])
    return pl.pallas_call(
        functools.partial(_decode_kernel, g_pages=g_pages, n_new=t, n_heads=n_heads, dh=dh),
        grid_spec=grid_spec,
        out_shape=jax.ShapeDtypeStruct((db, t, a), F32),
        compiler_params=_params("parallel", "arbitrary"),
        name="attn_decode",
    )(page_table.reshape(-1), qb, pad(k), pad(v), lfn,
      *([cache_kt] * g_pages), *([cache_vt] * g_pages), *([rin] * g_pages), *([tot] * g_pages))


def _pool_prompt_kernel(u_ref, halo_ref, o_ref, *, tm, pg):
    i = pl.program_id(1)
    u = u_ref[0]
    halo = jnp.where(i > 0, halo_ref[0], 0.0)
    ext = jnp.concatenate([halo, u], axis=0)
    pos = i * tm + lax.broadcasted_iota(jnp.int32, (tm, 1), 0)
    outs = []
    for g, w in enumerate(POOL_WINDOWS):
        r = ext[:, g * pg:(g + 1) * pg]
        k = 1
        while k < w:
            r = r + pltpu.roll(r, k, axis=0)
            k *= 2
        count = jnp.minimum(w, pos + 1).astype(F32)
        outs.append(r[POOL_HALO:] / count - u[:, g * pg:(g + 1) * pg])
    o_ref[0] = jnp.concatenate(outs, axis=-1).astype(o_ref.dtype)


def _pool_prompt(u):
    bsz, s, p = u.shape
    tm = min(s, 512)
    hb = tm // POOL_HALO
    return pl.pallas_call(
        functools.partial(_pool_prompt_kernel, tm=tm, pg=p // len(POOL_WINDOWS)),
        grid=(bsz, s // tm),
        in_specs=[pl.BlockSpec((1, tm, p), lambda b, i: (b, i, 0)),
                  pl.BlockSpec((1, POOL_HALO, p), lambda b, i: (b, jnp.maximum(i * hb - 1, 0), 0))],
        out_specs=pl.BlockSpec((1, tm, p), lambda b, i: (b, i, 0)),
        out_shape=jax.ShapeDtypeStruct(u.shape, BF16),
        compiler_params=_params("parallel", "parallel"),
        name="pool_prompt",
    )(u, u)


def _pool_sample_kernel(e_ref, o_ref, *, n_new, n_prev, pg):
    for t in range(n_new):
        e = n_prev + t
        outs = []
        for g, w in enumerate(POOL_WINDOWS):
            cols = slice(g * pg, (g + 1) * pg)
            lo = max(0, e - w + 1)
            acc = e_ref[lo, :, cols]
            for j in range(lo + 1, e + 1):
                acc = acc + e_ref[j, :, cols]
            outs.append(acc / float(min(w, e + 1)) - e_ref[e, :, cols])
        o_ref[t] = jnp.concatenate(outs, axis=-1).astype(o_ref.dtype)


def _pool_sample(ext_t, n_new):
    n_all, db, p = ext_t.shape
    return pl.pallas_call(
        functools.partial(_pool_sample_kernel, n_new=n_new, n_prev=n_all - n_new, pg=p // len(POOL_WINDOWS)),
        in_specs=[_vmem()], out_specs=_vmem(),
        out_shape=jax.ShapeDtypeStruct((n_new, db, p), BF16),
        compiler_params=pltpu.CompilerParams(vmem_limit_bytes=VMEM_LIMIT),
        name="pool_sample",
    )(ext_t)


def _merge_kernel(x_ref, g_ref, sh_ref, sc_ref, gt_ref, oa_ref, pl_ref, wpool_ref, ps_ref,
                  wpa_ref, wpb_ref, wga_ref, wgb_ref, wo_ref, o_ref):
    x = x_ref[0]
    h = _rms_mod(x, g_ref[...], sh_ref[0], sc_ref[0]).astype(BF16)
    pooled = pl_ref[0]
    n_groups, pg, _ = wpool_ref.shape
    mixed = jnp.concatenate(
        [_dot(pooled[:, g * pg:(g + 1) * pg], wpool_ref[g]) for g in range(n_groups)], axis=-1)
    o_pool = (mixed * ps_ref[...]).astype(BF16)
    y = jax.nn.sigmoid(_dot(h, wga_ref[...])) * _dot(oa_ref[0], wpa_ref[...])
    y = y + jax.nn.sigmoid(_dot(h, wgb_ref[...])) * _dot(o_pool, wpb_ref[...])
    o_ref[0] = x + gt_ref[0] * _dot(y.astype(BF16), wo_ref[...])


def _merge(x, g, shift, scale, gate, o_att, pooled, w_pool, pool_scale, w_pa, w_pb, w_ga, w_gb, w_o):
    bsz, s, d = x.shape
    tm = min(s, 256)
    nt = s // tm
    row = lambda n: pl.BlockSpec((1, tm, n), lambda b, i: (b, i, 0))
    return pl.pallas_call(
        _merge_kernel,
        grid=(bsz, nt),
        in_specs=[row(d), pl.BlockSpec((1, d), lambda b, i: (0, 0)),
                  _mod_spec(shift, tm, nt), _mod_spec(scale, tm, nt), _mod_spec(gate, tm, nt),
                  row(o_att.shape[2]), row(pooled.shape[2]),
                  _vmem(), _vmem(), _vmem(), _vmem(), _vmem(), _vmem(), _vmem()],
        out_specs=row(d),
        out_shape=jax.ShapeDtypeStruct(x.shape, F32),
        compiler_params=_params("parallel", "parallel"),
        name="merge",
    )(x, g, shift, scale, gate, o_att, pooled, w_pool, pool_scale, w_pa, w_pb, w_ga, w_gb, w_o)


def _route_kernel(x_ref, g_ref, sh_ref, sc_ref, wr_ref, rb_ref, h_ref, idx_ref, w_ref):
    h = _rms_mod(x_ref[0], g_ref[...], sh_ref[0], sc_ref[0])
    h_ref[0] = h.astype(BF16)
    n_exp = wr_ref.shape[0]
    tm = h.shape[0]
    per = n_exp // N_GROUPS
    scores = jax.nn.sigmoid(_dot_nt(wr_ref[...], h, precision=lax.Precision.HIGHEST))
    sel = scores + rb_ref[...]
    neg = -jnp.inf
    groups = [sel[g * per:(g + 1) * per] for g in range(N_GROUPS)]
    eidx = lax.broadcasted_iota(jnp.int32, (per, tm), 0)
    gscore = []
    for grp in groups:
        m1 = jnp.max(grp, axis=0, keepdims=True)
        first = jnp.min(jnp.where(grp == m1, eidx, per), axis=0, keepdims=True)
        gscore.append(m1 + jnp.max(jnp.where(eidx == first, neg, grp), axis=0, keepdims=True))
    cand = []
    for g in range(N_GROUPS):
        rank = jnp.zeros((1, tm), jnp.int32)
        for o in range(N_GROUPS):
            if o != g:
                ahead = gscore[o] >= gscore[g] if o < g else gscore[o] > gscore[g]
                rank = rank + ahead.astype(jnp.int32)
        cand.append(jnp.where(rank < TOPK_GROUPS, groups[g], neg))
    cand = jnp.concatenate(cand, axis=0)
    row = lax.broadcasted_iota(jnp.int32, (n_exp, tm), 0)
    idxs, wts = [], []
    for _ in range(TOP_K):
        m = jnp.max(cand, axis=0, keepdims=True)
        pick = jnp.min(jnp.where(cand == m, row, n_exp), axis=0, keepdims=True)
        hit = row == pick
        idxs.append(pick)
        wts.append(jnp.sum(jnp.where(hit, scores, 0.0), axis=0, keepdims=True))
        cand = jnp.where(hit, neg, cand)
    wts = jnp.concatenate(wts, axis=0)
    idx_ref[...] = jnp.concatenate(idxs, axis=0)
    w_ref[...] = wts / jnp.sum(wts, axis=0, keepdims=True) * ROUTED_SCALE


def _route(x, g, shift, scale, w_router_t, router_bias):
    bsz, s, d = x.shape
    tm = min(s, 256)
    nt = s // tm
    col = pl.BlockSpec((TOP_K, tm), lambda b, i: (0, b * nt + i))
    return pl.pallas_call(
        _route_kernel,
        grid=(bsz, nt),
        in_specs=[pl.BlockSpec((1, tm, d), lambda b, i: (b, i, 0)), pl.BlockSpec((1, d), lambda b, i: (0, 0)),
                  _mod_spec(shift, tm, nt), _mod_spec(scale, tm, nt), _vmem(), _vmem()],
        out_specs=[pl.BlockSpec((1, tm, d), lambda b, i: (b, i, 0)), col, col],
        out_shape=[jax.ShapeDtypeStruct(x.shape, BF16),
                   jax.ShapeDtypeStruct((TOP_K, bsz * s), jnp.int32),
                   jax.ShapeDtypeStruct((TOP_K, bsz * s), F32)],
        compiler_params=_params("parallel", "parallel"),
        name="route",
    )(x, g, shift, scale, w_router_t, router_bias)


def _moe_kernel(blk_ref, exp_ref, lo_ref, hi_ref, lead_ref, nxt_ref, par_ref, xs_ref, wt_ref,
                wg_hbm, wu_hbm, wd_hbm, o_ref, acc_ref, wg_f, wu_f, wd_f, wg_b, wu_b, wd_b, sems):
    i = pl.program_id(0)
    lo = lo_ref[i]
    hi = hi_ref[i]

    def fetch(expert, slot):
        return [pltpu.make_async_copy(src.at[expert], dst.at[slot], sems.at[n, slot])
                for n, (src, dst) in enumerate(((wg_hbm, wg_f), (wu_hbm, wu_f), (wd_hbm, wd_f)))]

    @pl.when(i == 0)
    def _():
        for c in fetch(exp_ref[0], 0):
            c.start()

    @pl.when(lead_ref[i] == 1)
    def _():
        slot = par_ref[i]
        nxt = nxt_ref[i]

        @pl.when(nxt >= 0)
        def _():
            for c in fetch(nxt, 1 - slot):
                c.start()

        for c in fetch(exp_ref[i], slot):
            c.wait()
        wg_b[...] = wg_f[slot].astype(BF16)
        wu_b[...] = wu_f[slot].astype(BF16)
        wd_b[...] = wd_f[slot].astype(BF16)

    @pl.when(hi > lo)
    def _():
        xb = xs_ref[...]
        row = lax.broadcasted_iota(jnp.int32, wt_ref.shape, 0)
        wt = jnp.where((row >= lo) & (row < hi), wt_ref[...], 0.0)
        hid = (_silu(_dot(xb, wg_b[...])) * _dot(xb, wu_b[...])).astype(BF16)
        y = _dot(hid, wd_b[...]) * wt

        @pl.when(lo == 0)
        def _():
            acc_ref[...] = y

        @pl.when(lo > 0)
        def _():
            acc_ref[...] = acc_ref[...] + y

        o_ref[...] = acc_ref[...].astype(o_ref.dtype)


def _moe(xs, row_w, e_sorted, start, w_gate, w_up, w_down):
    n_assign, d = xs.shape
    n_exp, _, f = w_gate.shape
    assert n_assign % MOE_BLOCK == 0
    n_blocks = n_assign // MOE_BLOCK
    cuts = jnp.sort(jnp.concatenate([jnp.arange(n_blocks, dtype=jnp.int32) * MOE_BLOCK, start[1:]]))
    n_items = cuts.shape[0]
    ends = jnp.concatenate([cuts[1:], jnp.full((1,), n_assign, jnp.int32)])
    blk = jnp.minimum(cuts // MOE_BLOCK, n_blocks - 1)
    expert = e_sorted[jnp.minimum(cuts, n_assign - 1)]
    lo, hi = cuts - blk * MOE_BLOCK, ends - blk * MOE_BLOCK
    live = ends > cuts
    item = jnp.arange(n_items, dtype=jnp.int32)
    change = jnp.concatenate([jnp.ones((1,), bool), expert[1:] != expert[:-1]])
    seen = jnp.cumsum(live.astype(jnp.int32))
    seen_at_run_start = lax.cummax(jnp.where(change, seen - live, 0))
    lead = live & (seen - seen_at_run_start == 1)
    parity = (jnp.cumsum(lead.astype(jnp.int32)) - 1) % 2
    lead_at = jnp.where(lead, item, n_items)
    nxt_item = lax.cummin(jnp.concatenate([lead_at[1:], jnp.full((1,), n_items, jnp.int32)]), reverse=True)
    nxt = jnp.where(nxt_item < n_items, expert[jnp.minimum(nxt_item, n_items - 1)], -1)
    row = lambda n: pl.BlockSpec((MOE_BLOCK, n), lambda i, blk, *_: (blk[i], 0))
    hbm = pl.BlockSpec(memory_space=pl.ANY)
    grid_spec = pltpu.PrefetchScalarGridSpec(
        num_scalar_prefetch=7, grid=(n_items,),
        in_specs=[row(d), row(1), hbm, hbm, hbm],
        out_specs=row(d),
        scratch_shapes=[pltpu.VMEM((MOE_BLOCK, d), F32),
                        pltpu.VMEM((2, d, f), F32), pltpu.VMEM((2, d, f), F32), pltpu.VMEM((2, f, d), F32),
                        pltpu.VMEM((d, f), BF16), pltpu.VMEM((d, f), BF16), pltpu.VMEM((f, d), BF16),
                        pltpu.SemaphoreType.DMA((3, 2))])
    i32 = lambda z: z.astype(jnp.int32)
    return pl.pallas_call(
        _moe_kernel, grid_spec=grid_spec,
        out_shape=jax.ShapeDtypeStruct((n_assign, d), BF16),
        compiler_params=_params("arbitrary"),
        name="moe",
    )(i32(blk), i32(expert), i32(lo), i32(hi), i32(lead), i32(nxt), i32(parity),
      xs, row_w, w_gate, w_up, w_down)


def _final_kernel(x_ref, h_ref, r_ref, gt_ref, wg_ref, wu_ref, wd_ref, gf_ref, o_ref):
    d = x_ref.shape[2]
    hb = h_ref[0]
    hid = (_silu(_dot(hb, wg_ref[...])) * _dot(hb, wu_ref[...])).astype(BF16)
    moe = _dot(hid, wd_ref[...])
    for k in range(r_ref.shape[1] // d):
        moe = moe + r_ref[:, k * d:(k + 1) * d].astype(F32)
    o_ref[0] = _rms(x_ref[0] + gt_ref[0] * moe, gf_ref[...])


def _final(x, h, routed, row0, gate, w_sg, w_su, w_sd, g_final):
    bsz, s, d = x.shape
    tm = min(s, 256)
    nt = s // tm
    assert row0 % tm == 0
    rb0 = row0 // tm
    row = pl.BlockSpec((1, tm, d), lambda b, i: (b, i, 0))
    return pl.pallas_call(
        _final_kernel,
        grid=(bsz, nt),
        in_specs=[row, row, pl.BlockSpec((tm, routed.shape[1]), lambda b, i: (rb0 + b * nt + i, 0)),
                  _mod_spec(gate, tm, nt), _vmem(), _vmem(), _vmem(), pl.BlockSpec((1, d), lambda b, i: (0, 0))],
        out_specs=row,
        out_shape=jax.ShapeDtypeStruct(x.shape, F32),
        compiler_params=_params("parallel", "parallel"),
        name="final",
    )(x, h, routed, gate, w_sg, w_su, w_sd, g_final)


def kernel(x_prompt, x_sample, cache_k, cache_v, cache_logf, state_pool, page_table, c_prompt, c_sample, w_ada, b_ada, g_mix, w_in, b_f, w_pool, pool_scale, w_pa, w_pb, w_o, g_ffn, w_router, router_bias, w_gate, w_up, w_down, w_sh_gate, w_sh_up, w_sh_down, g_final):
    depth = w_ada.shape[0]
    assert depth == 1
    bsz, seq, d = x_prompt.shape
    db, t_new, _ = x_sample.shape
    _, n_pool_pages, page, n_heads, dh = cache_k.shape
    a = n_heads * dh
    p = state_pool.shape[3]
    n_exp = w_router.shape[2]
    n_dec = db * t_new
    qscale = dh ** -0.5

    w = w_in[0]
    cuts = [a, 2 * a, 3 * a, 3 * a + n_heads, 3 * a + n_heads + p, 3 * a + n_heads + p + d]
    wq, wk, wv, wf, wu, wga, wgb = [z.astype(BF16) for z in jnp.split(w, cuts, axis=1)]
    wf = jnp.pad(wf, ((0, 0), (0, LANES - n_heads)))
    bf = jnp.pad(b_f[0], (0, LANES - n_heads)).reshape(1, LANES)
    wpool, wpa, wpb, wo = w_pool[0].astype(BF16), w_pa[0].astype(BF16), w_pb[0].astype(BF16), w_o[0].astype(BF16)
    wsg, wsu, wsd = w_sh_gate[0].astype(BF16), w_sh_up[0].astype(BF16), w_sh_down[0].astype(BF16)
    wr_t = w_router[0].T
    rbias = router_bias[0].reshape(n_exp, 1)
    gmix, gffn, gfin = g_mix[0].reshape(1, d), g_ffn[0].reshape(1, d), g_final.reshape(1, d)
    pscale = pool_scale[0].reshape(1, p)

    c_all = jnp.concatenate([c_prompt, c_sample], axis=0)
    r_pad = -c_all.shape[0] % 16
    mod = _ada(jnp.pad(c_all, ((0, r_pad), (0, 0))), w_ada[0], b_ada[0])
    mod_p = [m.reshape(bsz, 1, d) for m in jnp.split(mod[:bsz], 6, axis=-1)]
    mod_s = [jnp.repeat(m, t_new, axis=0).reshape(1, n_dec, d) for m in jnp.split(mod[bsz:bsz + db], 6, axis=-1)]

    def mixer_in(x, m):
        return _inproj(x, gmix, m[0], m[1], wq, wk, wv, wu, wf, bf, qscale)

    def mixer_out(x, m, o_att, pooled):
        x1 = _merge(x, gmix, m[0], m[1], m[2], o_att, pooled, wpool, pscale, wpa, wpb, wga, wgb, wo)
        return (x1,) + tuple(_route(x1, gffn, m[3], m[4], wr_t, rbias))

    q_p, k_p, v_p, u_p, lf_p = mixer_in(x_prompt, mod_p)
    logf_p = lf_p[:, :, :n_heads]
    cum_p = _cumsum_rows(logf_p.transpose(0, 2, 1))
    o_att_p = _attn_prompt(q_p, k_p, v_p, cum_p, dh)
    x1_p, h2_p, idx_p, wt_p = mixer_out(x_prompt, mod_p, o_att_p, _pool_prompt(u_p))

    xs3 = x_sample.reshape(1, n_dec, d)
    q_s, k_s, v_s, u_s, lf_s = mixer_in(xs3, mod_s)
    logf_s = lf_s[0, :, :n_heads].reshape(db, t_new, n_heads)
    u_s = u_s.reshape(db, t_new, p)
    cache_kt = cache_k[0].transpose(0, 2, 3, 1).reshape(n_pool_pages, a, page)
    cache_vt = cache_v[0].transpose(0, 2, 3, 1).reshape(n_pool_pages, a, page)
    o_att_s = _decode_attn(q_s.reshape(db, t_new, a), k_s.reshape(db, t_new, a), v_s.reshape(db, t_new, a),
                           logf_s, cache_kt, cache_vt, cache_logf[0].transpose(0, 2, 1), page_table, n_heads, dh)
    ext_s = jnp.concatenate([state_pool[0], u_s], axis=1)
    pooled_s = _pool_sample(ext_s.transpose(1, 0, 2), t_new).transpose(1, 0, 2).reshape(1, n_dec, p)
    x1_s, h2_s, idx_s, wt_s = mixer_out(xs3, mod_s, o_att_s.astype(BF16).reshape(1, n_dec, a), pooled_s)

    n_prompt = bsz * seq
    n_tok = n_prompt + n_dec
    h2_all = jnp.concatenate([h2_p.reshape(n_prompt, d), h2_s.reshape(n_dec, d)], axis=0)
    flat_e = jnp.concatenate([idx_p, idx_s], axis=1).T.reshape(-1)
    flat_w = jnp.concatenate([wt_p, wt_s], axis=1).T.reshape(-1)
    n_assign = n_tok * TOP_K
    ids = jnp.arange(n_assign, dtype=jnp.int32)
    e_sorted, order, w_sorted = lax.sort((flat_e, ids, flat_w), num_keys=1, is_stable=True)
    _, rank = lax.sort((order, ids), num_keys=1)
    start = jnp.searchsorted(e_sorted, jnp.arange(n_exp, dtype=jnp.int32), side='left').astype(jnp.int32)
    xs = h2_all[order // TOP_K]
    ys = _moe(xs, w_sorted.reshape(n_assign, 1), e_sorted, start, w_gate[0], w_up[0], w_down[0])
    routed = ys[rank].reshape(n_tok, TOP_K * d)

    y_p = _final(x1_p, h2_p, routed, 0, mod_p[5], wsg, wsu, wsd, gfin)
    y_s = _final(x1_s, h2_s, routed, n_prompt, mod_s[5], wsg, wsu, wsd, gfin)

    heads = lambda z, b, s: z.reshape(1, b, s, n_heads, dh)
    pool_p = u_p[:, seq - (POOL_HALO - 1):][None]
    pool_s = ext_s[:, ext_s.shape[1] - (POOL_HALO - 1):][None]
    return (y_p, y_s.reshape(db, t_new, d), heads(k_p, bsz, seq), heads(v_p, bsz, seq), logf_p[None], pool_p,
            heads(k_s, db, t_new), heads(v_s, db, t_new), logf_s[None], pool_s)
```

```python
import functools

import jax
import jax.numpy as jnp
from jax import lax
from jax.experimental import pallas as pl
from jax.experimental.pallas import tpu as pltpu

F32 = jnp.float32
BF16 = jnp.bfloat16

RMS_EPS = 1e-6
LOG2E = 1.4426950408889634
TOP_K = 8
N_GROUPS = 8
TOPK_GROUPS = 4
ROUTED_SCALE = 2.5
POOL_WINDOWS = (2, 4, 8, 16)
POOL_HALO = 16
LANES = 128
MOE_BLOCK = 128
VMEM_LIMIT = 56 * 1024 * 1024

_dot = functools.partial(jnp.dot, preferred_element_type=F32)


def _dot_nt(a, b, precision=None):
    return lax.dot_general(a, b, (((1,), (1,)), ((), ())), preferred_element_type=F32, precision=precision)


def _params(*sem):
    return pltpu.CompilerParams(dimension_semantics=sem, vmem_limit_bytes=VMEM_LIMIT)


def _vmem():
    return pl.BlockSpec(memory_space=pltpu.VMEM)


def _rms(x, g):
    return x * lax.rsqrt(jnp.mean(x * x, axis=-1, keepdims=True) + RMS_EPS) * g


def _rms_mod(x, g, shift, scale):
    return _rms(x, g) * (1.0 + scale) + shift


def _log_sigmoid(x):
    return jnp.minimum(x, 0.0) - jnp.log1p(jnp.exp(-jnp.abs(x)))


def _silu(x):
    return x * jax.nn.sigmoid(x)


def _cumsum_lanes(x):
    n = x.shape[-1]
    lane = lax.broadcasted_iota(jnp.int32, x.shape, x.ndim - 1)
    k = 1
    while k < n:
        x = x + jnp.where(lane >= k, pltpu.roll(x, k, axis=x.ndim - 1), 0.0)
        k *= 2
    return x


def _mod_spec(mod, tm, nt):
    _, r, d = mod.shape
    if r == 1:
        return pl.BlockSpec((1, 1, d), lambda b, i: (b, 0, 0))
    return pl.BlockSpec((1, tm, d), lambda b, i: (b, i, 0))


def _ada_kernel(c_ref, w_ref, b_ref, o_ref):
    a = _silu(c_ref[...]).astype(BF16)
    o_ref[...] = _dot(a, w_ref[...].astype(BF16)) + b_ref[...]


def _ada(c, w, b):
    r, d = c.shape
    n = w.shape[1]
    tn = min(n, 1024)
    return pl.pallas_call(
        _ada_kernel,
        grid=(n // tn,),
        in_specs=[pl.BlockSpec((r, d), lambda j: (0, 0)),
                  pl.BlockSpec((d, tn), lambda j: (0, j)),
                  pl.BlockSpec((1, tn), lambda j: (0, j))],
        out_specs=pl.BlockSpec((r, tn), lambda j: (0, j)),
        out_shape=jax.ShapeDtypeStruct((r, n), F32),
        compiler_params=_params("parallel"),
        name="ada",
    )(c, w, b.reshape(1, n))


def _inproj_kernel(x_ref, g_ref, sh_ref, sc_ref, wqt_ref, wk_ref, wvt_ref, wu_ref, wf_ref, bf_ref,
                   qt_ref, k_ref, vt_ref, u_ref, lf_ref, *, qscale):
    h = _rms_mod(x_ref[0], g_ref[...], sh_ref[0], sc_ref[0]).astype(BF16)
    qt_ref[0] = (_dot_nt(wqt_ref[...], h) * qscale).astype(BF16)
    vt_ref[0] = _dot_nt(wvt_ref[...], h)
    k_ref[0] = _dot(h, wk_ref[...])
    u_ref[0] = _dot(h, wu_ref[...])
    lf_ref[0] = _log_sigmoid(_dot(h, wf_ref[...]) + bf_ref[...])


def _inproj(x, g, shift, scale, wq_t, wk, wv_t, wu, wf, bf, qscale):
    bsz, s, d = x.shape
    a, p = wk.shape[1], wu.shape[1]
    tm = min(s, 512)
    nt = s // tm
    row = lambda n: pl.BlockSpec((1, tm, n), lambda b, i: (b, i, 0))
    col = pl.BlockSpec((1, a, tm), lambda b, i: (b, 0, i))
    return pl.pallas_call(
        functools.partial(_inproj_kernel, qscale=qscale),
        grid=(bsz, nt),
        in_specs=[row(d), pl.BlockSpec((1, d), lambda b, i: (0, 0)),
                  _mod_spec(shift, tm, nt), _mod_spec(scale, tm, nt),
                  _vmem(), _vmem(), _vmem(), _vmem(), _vmem(), _vmem()],
        out_specs=[col, row(a), col, row(p), row(LANES)],
        out_shape=[jax.ShapeDtypeStruct((bsz, a, s), BF16),
                   jax.ShapeDtypeStruct((bsz, s, a), F32),
                   jax.ShapeDtypeStruct((bsz, a, s), F32),
                   jax.ShapeDtypeStruct((bsz, s, p), F32),
                   jax.ShapeDtypeStruct((bsz, s, LANES), F32)],
        compiler_params=_params("parallel", "parallel"),
        name="inproj",
    )(x, g, shift, scale, wq_t, wk, wv_t, wu, wf, bf)


def _cumsum_kernel(x_ref, o_ref):
    o_ref[0] = _cumsum_lanes(x_ref[0])


def _cumsum_rows(x):
    bsz, h, s = x.shape
    spec = pl.BlockSpec((1, h, s), lambda b: (b, 0, 0))
    return pl.pallas_call(
        _cumsum_kernel, grid=(bsz,), in_specs=[spec], out_specs=spec,
        out_shape=jax.ShapeDtypeStruct(x.shape, F32),
        compiler_params=_params("parallel"), name="logf_cumsum",
    )(x)


def _attn_prompt_kernel(qt_ref, k_ref, vt_ref, c_ref, o_ref, m_ref, l_ref, acc_ref, *, t, dh):
    qi = pl.program_id(2)
    qt = qt_ref[0]
    row = lax.broadcasted_iota(jnp.int32, (2 * dh, 1), 0)
    zero = jnp.zeros_like(qt)
    qh = (jnp.where(row < dh, qt, zero), jnp.where(row < dh, zero, qt))
    m_ref[...] = jnp.full(m_ref.shape, -jnp.inf, F32)
    l_ref[...] = jnp.zeros(l_ref.shape, F32)
    acc_ref[...] = jnp.zeros(acc_ref.shape, F32)

    def tile(ki, diagonal):
        k0 = pl.multiple_of(ki * t, t)
        kb = k_ref[0, pl.ds(k0, t), :].astype(BF16)
        vt = vt_ref[0, :, pl.ds(k0, t)].astype(BF16)
        cb = c_ref[0, 0, ki] * LOG2E
        pv, corr = [], []
        for j in range(2):
            s = _dot(kb, qh[j]) - cb[:, j:j + 1]
            if diagonal:
                key = lax.broadcasted_iota(jnp.int32, (t, t), 0)
                qry = lax.broadcasted_iota(jnp.int32, (t, t), 1)
                s = jnp.where(key <= qry, s, -jnp.inf)
            m_old = m_ref[j]
            m_new = jnp.maximum(m_old, jnp.max(s, axis=0, keepdims=True))
            p = jnp.exp2(s - m_new)
            cj = jnp.exp2(m_old - m_new)
            l_ref[j] = l_ref[j] * cj + jnp.sum(p, axis=0, keepdims=True)
            m_ref[j] = m_new
            pv.append(_dot(vt[j * dh:(j + 1) * dh], p.astype(BF16)))
            corr.append(jnp.broadcast_to(cj, (dh, t)))
        acc_ref[...] = acc_ref[...] * jnp.concatenate(corr, axis=0) + jnp.concatenate(pv, axis=0)

    def body(ki, carry):
        tile(ki, False)
        return carry

    lax.fori_loop(0, qi, body, 0)
    tile(qi, True)
    denom = jnp.concatenate([jnp.broadcast_to(l_ref[j], (dh, t)) for j in range(2)], axis=0)
    o_ref[0] = (acc_ref[...] / denom).T.astype(o_ref.dtype)


def _attn_prompt(qt, k, vt, cum, dh):
    bsz, s, a = k.shape
    hp = a // (2 * dh)
    t = min(s, 512)
    nt = s // t
    cum = cum.reshape(bsz, hp, 2, nt, t).transpose(0, 1, 3, 4, 2)
    return pl.pallas_call(
        functools.partial(_attn_prompt_kernel, t=t, dh=dh),
        grid=(bsz, hp, nt),
        in_specs=[pl.BlockSpec((1, 2 * dh, t), lambda b, h, i: (b, h, i)),
                  pl.BlockSpec((1, s, 2 * dh), lambda b, h, i: (b, 0, h)),
                  pl.BlockSpec((1, 2 * dh, s), lambda b, h, i: (b, h, 0)),
                  pl.BlockSpec((1, 1, nt, t, 2), lambda b, h, i: (b, h, 0, 0, 0))],
        out_specs=pl.BlockSpec((1, t, 2 * dh), lambda b, h, i: (b, i, h)),
        out_shape=jax.ShapeDtypeStruct((bsz, s, a), BF16),
        scratch_shapes=[pltpu.VMEM((2, 1, t), F32), pltpu.VMEM((2, 1, t), F32),
                        pltpu.VMEM((2 * dh, t), F32)],
        compiler_params=_params("parallel", "parallel", "parallel"),
        name="attn_prompt",
    )(qt, k, vt, cum)


def _logf_pages_kernel(x_ref, rin_ref, tot_ref):
    x = x_ref[...]
    n = x.shape[1]
    lane = lax.broadcasted_iota(jnp.int32, x.shape, 1)
    sfx = x
    k = 1
    while k < n:
        sfx = sfx + jnp.where(lane + k < n, pltpu.roll(sfx, n - k, axis=1), 0.0)
        k *= 2
    rin_ref[...] = sfx - x
    tot_ref[...] = jnp.broadcast_to(sfx[:, 0:1], x.shape)


def _logf_pages(lf_t):
    n_pool, n_heads, page = lf_t.shape
    pb = 64 if n_pool % 64 == 0 else n_pool
    spec = pl.BlockSpec((pb * n_heads, page), lambda i: (i, 0))
    rin, tot = pl.pallas_call(
        _logf_pages_kernel,
        grid=(n_pool // pb,), in_specs=[spec], out_specs=[spec, spec],
        out_shape=[jax.ShapeDtypeStruct((n_pool * n_heads, page), F32)] * 2,
        compiler_params=_params("parallel"), name="logf_pages",
    )(lf_t.reshape(n_pool * n_heads, page))
    return rin.reshape(lf_t.shape), tot.reshape(lf_t.shape)


def _decode_kernel(pt_ref, qb_ref, kn_ref, vn_ref, lfn_ref, *refs, g_pages, n_new, n_heads, dh):
    k_refs = refs[:g_pages]
    v_refs = refs[g_pages:2 * g_pages]
    rin_refs = refs[2 * g_pages:3 * g_pages]
    tot_refs = refs[3 * g_pages:4 * g_pages]
    o_ref, m_ref, l_ref, acc_ref, carry_ref = refs[4 * g_pages:]
    step = pl.program_id(1)
    qb = qb_ref[0]

    @pl.when(step == 0)
    def _():
        c = _cumsum_lanes(lfn_ref[0])
        tot = c[:, LANES - 1:LANES]
        rnew = (tot - c)[:, :16]
        s = _dot_nt(qb, kn_ref[0]) + jnp.concatenate([rnew] * n_new, axis=0)
        tok = lax.broadcasted_iota(jnp.int32, s.shape, 0) // n_heads
        key = lax.broadcasted_iota(jnp.int32, s.shape, 1)
        s = jnp.where(key <= tok, s, -jnp.inf)
        m = jnp.max(s, axis=-1, keepdims=True)
        p = jnp.exp(s - m)
        m_ref[...] = m
        l_ref[...] = jnp.sum(p, axis=-1, keepdims=True)
        acc_ref[...] = _dot(p.astype(BF16), vn_ref[0])
        carry_ref[...] = jnp.broadcast_to(tot, carry_ref.shape)

    carry = carry_ref[...]
    scores = []
    for g in range(g_pages):
        r = carry + rin_refs[g][0]
        carry = carry + tot_refs[g][0]
        kb = k_refs[g][0].astype(BF16)
        scores.append(_dot(qb, kb) + jnp.concatenate([r] * n_new, axis=0))
    carry_ref[...] = carry
    s = jnp.concatenate(scores, axis=1)
    m_old = m_ref[...]
    m_new = jnp.maximum(m_old, jnp.max(s, axis=-1, keepdims=True))
    p = jnp.exp(s - m_new)
    corr = jnp.exp(m_old - m_new)
    l_ref[...] = l_ref[...] * corr + jnp.sum(p, axis=-1, keepdims=True)
    m_ref[...] = m_new
    pb = p.astype(BF16)
    page = s.shape[1] // g_pages
    pv = _dot_nt(pb[:, :page], v_refs[0][0].astype(BF16))
    for g in range(1, g_pages):
        pv = pv + _dot_nt(pb[:, g * page:(g + 1) * page], v_refs[g][0].astype(BF16))
    acc_ref[...] = acc_ref[...] * corr + pv

    @pl.when(step == pl.num_programs(1) - 1)
    def _():
        o = acc_ref[...] / l_ref[...]
        head = lax.broadcasted_iota(jnp.int32, o.shape, 0) % n_heads
        lane_head = lax.broadcasted_iota(jnp.int32, o.shape, 1) // dh
        o = jnp.where(head == lane_head, o, 0.0)
        o_ref[0] = jnp.sum(o.reshape(n_new, n_heads, o.shape[1]), axis=1)


def _decode_attn(q, k, v, logf, cache_kt, cache_vt, cache_logf_t, page_table, n_heads, dh):
    db, t, a = q.shape
    n_pages = page_table.shape[1]
    page = cache_kt.shape[2]
    assert page == LANES and t <= 16
    g_pages = 8 if n_pages % 8 == 0 else (4 if n_pages % 4 == 0 else 1)
    rows = t * n_heads
    rin, tot = _logf_pages(cache_logf_t)
    eye = jnp.eye(n_heads, dtype=BF16)
    qb = (q.reshape(db, t, 1, n_heads, dh) * eye[None, None, :, :, None]).reshape(db, rows, a)
    pad = lambda z: jnp.pad(z.astype(BF16), ((0, 0), (0, 16 - t), (0, 0)))
    lfn = jnp.pad(logf.transpose(0, 2, 1), ((0, 0), (0, 0), (0, LANES - t)))

    def page_idx(g):
        return lambda b, s, pt: (pt[b * n_pages + n_pages - 1 - (s * g_pages + g)], 0, 0)

    seq = lambda shape: pl.BlockSpec((1,) + shape, lambda b, s, pt: (b, 0, 0))
    in_specs = [seq((rows, a)), seq((16, a)), seq((16, a)), seq((n_heads, LANES))]
    for shape in ((1, a, page), (1, a, page), (1, n_heads, page), (1, n_heads, page)):
        in_specs += [pl.BlockSpec(shape, page_idx(g)) for g in range(g_pages)]
    grid_spec = pltpu.PrefetchScalarGridSpec(
        num_scalar_prefetch=1, grid=(db, n_pages // g_pages), in_specs=in_specs,
        out_specs=seq((t, a)),
        scratch_shapes=[pltpu.VMEM((rows, 1), F32), pltpu.VMEM((rows, 1), F32),
                        pltpu.VMEM((rows, a), F32), pltpu.VMEM((n_heads, page), F32)])
    return pl.pallas_call(
        functools.partial(_decode_kernel, g_pages=g_pages, n_new=t, n_heads=n_heads, dh=dh),
        grid_spec=grid_spec,
        out_shape=jax.ShapeDtypeStruct((db, t, a), F32),
        compiler_params=_params("parallel", "arbitrary"),
        name="attn_decode",
    )(page_table.reshape(-1), qb, pad(k), pad(v), lfn,
      *([cache_kt] * g_pages), *([cache_vt] * g_pages), *([rin] * g_pages), *([tot] * g_pages))


def _pool_prompt_kernel(u_ref, halo_ref, o_ref, *, tm, pg):
    i = pl.program_id(1)
    u = u_ref[0]
    halo = jnp.where(i > 0, halo_ref[0], 0.0)
    ext = jnp.concatenate([halo, u], axis=0)
    pos = i * tm + lax.broadcasted_iota(jnp.int32, (tm, 1), 0)
    outs = []
    for g, w in enumerate(POOL_WINDOWS):
        r = ext[:, g * pg:(g + 1) * pg]
        k = 1
        while k < w:
            r = r + pltpu.roll(r, k, axis=0)
            k *= 2
        count = jnp.minimum(w, pos + 1).astype(F32)
        outs.append(r[POOL_HALO:] / count - u[:, g * pg:(g + 1) * pg])
    o_ref[0] = jnp.concatenate(outs, axis=-1).astype(o_ref.dtype)


def _pool_prompt(u):
    bsz, s, p = u.shape
    tm = min(s, 512)
    hb = tm // POOL_HALO
    return pl.pallas_call(
        functools.partial(_pool_prompt_kernel, tm=tm, pg=p // len(POOL_WINDOWS)),
        grid=(bsz, s // tm),
        in_specs=[pl.BlockSpec((1, tm, p), lambda b, i: (b, i, 0)),
                  pl.BlockSpec((1, POOL_HALO, p), lambda b, i: (b, jnp.maximum(i * hb - 1, 0), 0))],
        out_specs=pl.BlockSpec((1, tm, p), lambda b, i: (b, i, 0)),
        out_shape=jax.ShapeDtypeStruct(u.shape, BF16),
        compiler_params=_params("parallel", "parallel"),
        name="pool_prompt",
    )(u, u)


def _pool_sample_kernel(e_ref, o_ref, *, n_new, n_prev, pg):
    for t in range(n_new):
        e = n_prev + t
        outs = []
        for g, w in enumerate(POOL_WINDOWS):
            cols = slice(g * pg, (g + 1) * pg)
            lo = max(0, e - w + 1)
            acc = e_ref[lo, :, cols]
            for j in range(lo + 1, e + 1):
                acc = acc + e_ref[j, :, cols]
            outs.append(acc / float(min(w, e + 1)) - e_ref[e, :, cols])
        o_ref[t] = jnp.concatenate(outs, axis=-1).astype(o_ref.dtype)


def _pool_sample(ext_t, n_new):
    n_all, db, p = ext_t.shape
    return pl.pallas_call(
        functools.partial(_pool_sample_kernel, n_new=n_new, n_prev=n_all - n_new, pg=p // len(POOL_WINDOWS)),
        in_specs=[_vmem()], out_specs=_vmem(),
        out_shape=jax.ShapeDtypeStruct((n_new, db, p), BF16),
        compiler_params=pltpu.CompilerParams(vmem_limit_bytes=VMEM_LIMIT),
        name="pool_sample",
    )(ext_t)


def _merge_kernel(x_ref, g_ref, sh_ref, sc_ref, gt_ref, oa_ref, pl_ref, wpool_ref, ps_ref,
                  wpa_ref, wpb_ref, wga_ref, wgb_ref, wo_ref, o_ref):
    x = x_ref[0]
    h = _rms_mod(x, g_ref[...], sh_ref[0], sc_ref[0]).astype(BF16)
    pooled = pl_ref[0]
    n_groups, pg, _ = wpool_ref.shape
    mixed = jnp.concatenate(
        [_dot(pooled[:, g * pg:(g + 1) * pg], wpool_ref[g]) for g in range(n_groups)], axis=-1)
    o_pool = (mixed * ps_ref[...]).astype(BF16)
    y = jax.nn.sigmoid(_dot(h, wga_ref[...])) * _dot(oa_ref[0], wpa_ref[...])
    y = y + jax.nn.sigmoid(_dot(h, wgb_ref[...])) * _dot(o_pool, wpb_ref[...])
    o_ref[0] = x + gt_ref[0] * _dot(y.astype(BF16), wo_ref[...])


def _merge(x, g, shift, scale, gate, o_att, pooled, w_pool, pool_scale, w_pa, w_pb, w_ga, w_gb, w_o):
    bsz, s, d = x.shape
    tm = min(s, 256)
    nt = s // tm
    row = lambda n: pl.BlockSpec((1, tm, n), lambda b, i: (b, i, 0))
    return pl.pallas_call(
        _merge_kernel,
        grid=(bsz, nt),
        in_specs=[row(d), pl.BlockSpec((1, d), lambda b, i: (0, 0)),
                  _mod_spec(shift, tm, nt), _mod_spec(scale, tm, nt), _mod_spec(gate, tm, nt),
                  row(o_att.shape[2]), row(pooled.shape[2]),
                  _vmem(), _vmem(), _vmem(), _vmem(), _vmem(), _vmem(), _vmem()],
        out_specs=row(d),
        out_shape=jax.ShapeDtypeStruct(x.shape, F32),
        compiler_params=_params("parallel", "parallel"),
        name="merge",
    )(x, g, shift, scale, gate, o_att, pooled, w_pool, pool_scale, w_pa, w_pb, w_ga, w_gb, w_o)


def _route_kernel(x_ref, g_ref, sh_ref, sc_ref, wr_ref, rb_ref, h_ref, idx_ref, w_ref):
    h = _rms_mod(x_ref[0], g_ref[...], sh_ref[0], sc_ref[0])
    h_ref[0] = h.astype(BF16)
    n_exp = wr_ref.shape[0]
    tm = h.shape[0]
    per = n_exp // N_GROUPS
    scores = jax.nn.sigmoid(_dot_nt(wr_ref[...], h, precision=lax.Precision.HIGHEST))
    sel = scores + rb_ref[...]
    neg = -jnp.inf
    groups = [sel[g * per:(g + 1) * per] for g in range(N_GROUPS)]
    eidx = lax.broadcasted_iota(jnp.int32, (per, tm), 0)
    gscore = []
    for grp in groups:
        m1 = jnp.max(grp, axis=0, keepdims=True)
        first = jnp.min(jnp.where(grp == m1, eidx, per), axis=0, keepdims=True)
        gscore.append(m1 + jnp.max(jnp.where(eidx == first, neg, grp), axis=0, keepdims=True))
    cand = []
    for g in range(N_GROUPS):
        rank = jnp.zeros((1, tm), jnp.int32)
        for o in range(N_GROUPS):
            if o != g:
                ahead = gscore[o] >= gscore[g] if o < g else gscore[o] > gscore[g]
                rank = rank + ahead.astype(jnp.int32)
        cand.append(jnp.where(rank < TOPK_GROUPS, groups[g], neg))
    cand = jnp.concatenate(cand, axis=0)
    row = lax.broadcasted_iota(jnp.int32, (n_exp, tm), 0)
    idxs, wts = [], []
    for _ in range(TOP_K):
        m = jnp.max(cand, axis=0, keepdims=True)
        pick = jnp.min(jnp.where(cand == m, row, n_exp), axis=0, keepdims=True)
        hit = row == pick
        idxs.append(pick)
        wts.append(jnp.sum(jnp.where(hit, scores, 0.0), axis=0, keepdims=True))
        cand = jnp.where(hit, neg, cand)
    wts = jnp.concatenate(wts, axis=0)
    idx_ref[...] = jnp.concatenate(idxs, axis=0)
    w_ref[...] = wts / jnp.sum(wts, axis=0, keepdims=True) * ROUTED_SCALE


def _route(x, g, shift, scale, w_router_t, router_bias):
    bsz, s, d = x.shape
    tm = min(s, 256)
    nt = s // tm
    col = pl.BlockSpec((TOP_K, tm), lambda b, i: (0, b * nt + i))
    return pl.pallas_call(
        _route_kernel,
        grid=(bsz, nt),
        in_specs=[pl.BlockSpec((1, tm, d), lambda b, i: (b, i, 0)), pl.BlockSpec((1, d), lambda b, i: (0, 0)),
                  _mod_spec(shift, tm, nt), _mod_spec(scale, tm, nt), _vmem(), _vmem()],
        out_specs=[pl.BlockSpec((1, tm, d), lambda b, i: (b, i, 0)), col, col],
        out_shape=[jax.ShapeDtypeStruct(x.shape, BF16),
                   jax.ShapeDtypeStruct((TOP_K, bsz * s), jnp.int32),
                   jax.ShapeDtypeStruct((TOP_K, bsz * s), F32)],
        compiler_params=_params("parallel", "parallel"),
        name="route",
    )(x, g, shift, scale, w_router_t, router_bias)


def _moe_kernel(blk_ref, exp_ref, lo_ref, hi_ref, lead_ref, nxt_ref, par_ref, xs_ref, wt_ref,
                wg_hbm, wu_hbm, wd_hbm, o_ref, acc_ref, wg_f, wu_f, wd_f, wg_b, wu_b, wd_b, sems):
    i = pl.program_id(0)
    lo = lo_ref[i]
    hi = hi_ref[i]

    def fetch(expert, slot):
        return [pltpu.make_async_copy(src.at[expert], dst.at[slot], sems.at[n, slot])
                for n, (src, dst) in enumerate(((wg_hbm, wg_f), (wu_hbm, wu_f), (wd_hbm, wd_f)))]

    @pl.when(i == 0)
    def _():
        for c in fetch(exp_ref[0], 0):
            c.start()

    @pl.when(lead_ref[i] == 1)
    def _():
        slot = par_ref[i]
        nxt = nxt_ref[i]

        @pl.when(nxt >= 0)
        def _():
            for c in fetch(nxt, 1 - slot):
                c.start()

        for c in fetch(exp_ref[i], slot):
            c.wait()
        wg_b[...] = wg_f[slot].astype(BF16)
        wu_b[...] = wu_f[slot].astype(BF16)
        wd_b[...] = wd_f[slot].astype(BF16)

    @pl.when(hi > lo)
    def _():
        xb = xs_ref[...]
        row = lax.broadcasted_iota(jnp.int32, wt_ref.shape, 0)
        wt = jnp.where((row >= lo) & (row < hi), wt_ref[...], 0.0)
        hid = (_silu(_dot(xb, wg_b[...])) * _dot(xb, wu_b[...])).astype(BF16)
        y = _dot(hid, wd_b[...]) * wt

        @pl.when(lo == 0)
        def _():
            acc_ref[...] = y

        @pl.when(lo > 0)
        def _():
            acc_ref[...] = acc_ref[...] + y

        o_ref[...] = acc_ref[...].astype(o_ref.dtype)


def _moe(xs, row_w, e_sorted, start, w_gate, w_up, w_down):
    n_assign, d = xs.shape
    n_exp, _, f = w_gate.shape
    assert n_assign % MOE_BLOCK == 0
    n_blocks = n_assign // MOE_BLOCK
    cuts = jnp.sort(jnp.concatenate([jnp.arange(n_blocks, dtype=jnp.int32) * MOE_BLOCK, start[1:]]))
    n_items = cuts.shape[0]
    ends = jnp.concatenate([cuts[1:], jnp.full((1,), n_assign, jnp.int32)])
    blk = jnp.minimum(cuts // MOE_BLOCK, n_blocks - 1)
    expert = e_sorted[jnp.minimum(cuts, n_assign - 1)]
    lo, hi = cuts - blk * MOE_BLOCK, ends - blk * MOE_BLOCK
    live = ends > cuts
    item = jnp.arange(n_items, dtype=jnp.int32)
    change = jnp.concatenate([jnp.ones((1,), bool), expert[1:] != expert[:-1]])
    seen = jnp.cumsum(live.astype(jnp.int32))
    seen_at_run_start = lax.cummax(jnp.where(change, seen - live, 0))
    lead = live & (seen - seen_at_run_start == 1)
    parity = (jnp.cumsum(lead.astype(jnp.int32)) - 1) % 2
    lead_at = jnp.where(lead, item, n_items)
    nxt_item = lax.cummin(jnp.concatenate([lead_at[1:], jnp.full((1,), n_items, jnp.int32)]), reverse=True)
    nxt = jnp.where(nxt_item < n_items, expert[jnp.minimum(nxt_item, n_items - 1)], -1)
    row = lambda n: pl.BlockSpec((MOE_BLOCK, n), lambda i, blk, *_: (blk[i], 0))
    hbm = pl.BlockSpec(memory_space=pl.ANY)
    grid_spec = pltpu.PrefetchScalarGridSpec(
        num_scalar_prefetch=7, grid=(n_items,),
        in_specs=[row(d), row(1), hbm, hbm, hbm],
        out_specs=row(d),
        scratch_shapes=[pltpu.VMEM((MOE_BLOCK, d), F32),
                        pltpu.VMEM((2, d, f), F32), pltpu.VMEM((2, d, f), F32), pltpu.VMEM((2, f, d), F32),
                        pltpu.VMEM((d, f), BF16), pltpu.VMEM((d, f), BF16), pltpu.VMEM((f, d), BF16),
                        pltpu.SemaphoreType.DMA((3, 2))])
    i32 = lambda z: z.astype(jnp.int32)
    return pl.pallas_call(
        _moe_kernel, grid_spec=grid_spec,
        out_shape=jax.ShapeDtypeStruct((n_assign, d), BF16),
        compiler_params=_params("arbitrary"),
        name="moe",
    )(i32(blk), i32(expert), i32(lo), i32(hi), i32(lead), i32(nxt), i32(parity),
      xs, row_w, w_gate, w_up, w_down)


def _final_kernel(x_ref, h_ref, r_ref, gt_ref, wg_ref, wu_ref, wd_ref, gf_ref, o_ref):
    hb = h_ref[0]
    hid = (_silu(_dot(hb, wg_ref[...])) * _dot(hb, wu_ref[...])).astype(BF16)
    moe = _dot(hid, wd_ref[...])
    for k in range(r_ref.shape[0]):
        moe = moe + r_ref[k].astype(F32)
    o_ref[0] = _rms(x_ref[0] + gt_ref[0] * moe, gf_ref[...])


def _final(x, h, routed, row0, gate, w_sg, w_su, w_sd, g_final):
    bsz, s, d = x.shape
    tm = min(s, 256)
    nt = s // tm
    assert row0 % tm == 0
    rb0 = row0 // tm
    row = pl.BlockSpec((1, tm, d), lambda b, i: (b, i, 0))
    return pl.pallas_call(
        _final_kernel,
        grid=(bsz, nt),
        in_specs=[row, row, pl.BlockSpec((routed.shape[0], tm, d), lambda b, i: (0, rb0 + b * nt + i, 0)),
                  _mod_spec(gate, tm, nt), _vmem(), _vmem(), _vmem(), pl.BlockSpec((1, d), lambda b, i: (0, 0))],
        out_specs=row,
        out_shape=jax.ShapeDtypeStruct(x.shape, F32),
        compiler_params=_params("parallel", "parallel"),
        name="final",
    )(x, h, routed, gate, w_sg, w_su, w_sd, g_final)


def kernel(x_prompt, x_sample, cache_k, cache_v, cache_logf, state_pool, page_table, c_prompt, c_sample, w_ada, b_ada, g_mix, w_in, b_f, w_pool, pool_scale, w_pa, w_pb, w_o, g_ffn, w_router, router_bias, w_gate, w_up, w_down, w_sh_gate, w_sh_up, w_sh_down, g_final):
    depth = w_ada.shape[0]
    assert depth == 1
    bsz, seq, d = x_prompt.shape
    db, t_new, _ = x_sample.shape
    _, n_pool_pages, page, n_heads, dh = cache_k.shape
    a = n_heads * dh
    p = state_pool.shape[3]
    n_exp = w_router.shape[2]
    n_dec = db * t_new

    w = w_in[0]
    cuts = [a, 2 * a, 3 * a, 3 * a + n_heads, 3 * a + n_heads + p, 3 * a + n_heads + p + d]
    wq, wk, wv, wf, wu, wga, wgb = [z.astype(BF16) for z in jnp.split(w, cuts, axis=1)]
    wq_t, wv_t = wq.T, wv.T
    wf = jnp.pad(wf, ((0, 0), (0, LANES - n_heads)))
    bf = jnp.pad(b_f[0], (0, LANES - n_heads)).reshape(1, LANES)
    wpool, wpa, wpb, wo = w_pool[0].astype(BF16), w_pa[0].astype(BF16), w_pb[0].astype(BF16), w_o[0].astype(BF16)
    wsg, wsu, wsd = w_sh_gate[0].astype(BF16), w_sh_up[0].astype(BF16), w_sh_down[0].astype(BF16)
    wr_t = w_router[0].T
    rbias = router_bias[0].reshape(n_exp, 1)
    gmix, gffn, gfin = g_mix[0].reshape(1, d), g_ffn[0].reshape(1, d), g_final.reshape(1, d)
    pscale = pool_scale[0].reshape(1, p)

    c_all = jnp.concatenate([c_prompt, c_sample], axis=0)
    r_pad = -c_all.shape[0] % 16
    mod = _ada(jnp.pad(c_all, ((0, r_pad), (0, 0))), w_ada[0], b_ada[0])
    mod_p = [m.reshape(bsz, 1, d) for m in jnp.split(mod[:bsz], 6, axis=-1)]
    mod_s = [jnp.repeat(m, t_new, axis=0).reshape(1, n_dec, d) for m in jnp.split(mod[bsz:bsz + db], 6, axis=-1)]

    def mixer_in(x, m, qscale):
        return _inproj(x, gmix, m[0], m[1], wq_t, wk, wv_t, wu, wf, bf, qscale)

    def mixer_out(x, m, o_att, pooled):
        x1 = _merge(x, gmix, m[0], m[1], m[2], o_att, pooled, wpool, pscale, wpa, wpb, wga, wgb, wo)
        return (x1,) + tuple(_route(x1, gffn, m[3], m[4], wr_t, rbias))

    qt_p, k_p, vt_p, u_p, lf_p = mixer_in(x_prompt, mod_p, dh ** -0.5 * LOG2E)
    logf_p = lf_p[:, :, :n_heads]
    cum_p = _cumsum_rows(logf_p.transpose(0, 2, 1))
    o_att_p = _attn_prompt(qt_p, k_p, vt_p, cum_p, dh)
    x1_p, h2_p, idx_p, wt_p = mixer_out(x_prompt, mod_p, o_att_p, _pool_prompt(u_p))

    xs3 = x_sample.reshape(1, n_dec, d)
    qt_s, k_s, vt_s, u_s, lf_s = mixer_in(xs3, mod_s, dh ** -0.5)
    q_s, v_s = qt_s[0].T, vt_s[0].T
    logf_s = lf_s[0, :, :n_heads].reshape(db, t_new, n_heads)
    u_s = u_s.reshape(db, t_new, p)
    cache_kt = cache_k[0].transpose(0, 2, 3, 1).reshape(n_pool_pages, a, page)
    cache_vt = cache_v[0].transpose(0, 2, 3, 1).reshape(n_pool_pages, a, page)
    o_att_s = _decode_attn(q_s.reshape(db, t_new, a), k_s.reshape(db, t_new, a), v_s.reshape(db, t_new, a),
                           logf_s, cache_kt, cache_vt, cache_logf[0].transpose(0, 2, 1), page_table, n_heads, dh)
    ext_s = jnp.concatenate([state_pool[0], u_s], axis=1)
    pooled_s = _pool_sample(ext_s.transpose(1, 0, 2), t_new).transpose(1, 0, 2).reshape(1, n_dec, p)
    x1_s, h2_s, idx_s, wt_s = mixer_out(xs3, mod_s, o_att_s.astype(BF16).reshape(1, n_dec, a), pooled_s)

    n_prompt = bsz * seq
    n_tok = n_prompt + n_dec
    h2_all = jnp.concatenate([h2_p.reshape(n_prompt, d), h2_s.reshape(n_dec, d)], axis=0)
    flat_e = jnp.concatenate([idx_p, idx_s], axis=1).T.reshape(-1)
    flat_w = jnp.concatenate([wt_p, wt_s], axis=1).T.reshape(-1)
    n_assign = n_tok * TOP_K
    ids = jnp.arange(n_assign, dtype=jnp.int32)
    e_sorted, order, w_sorted = lax.sort((flat_e, ids, flat_w), num_keys=1, is_stable=True)
    _, rank = lax.sort((order, ids), num_keys=1)
    start = jnp.searchsorted(e_sorted, jnp.arange(n_exp, dtype=jnp.int32), side='left').astype(jnp.int32)
    xs = h2_all[order // TOP_K]
    ys = _moe(xs, w_sorted.reshape(n_assign, 1), e_sorted, start, w_gate[0], w_up[0], w_down[0])
    routed = ys[rank.reshape(n_tok, TOP_K).T.reshape(-1)].reshape(TOP_K, n_tok, d)

    y_p = _final(x1_p, h2_p, routed, 0, mod_p[5], wsg, wsu, wsd, gfin)
    y_s = _final(x1_s, h2_s, routed, n_prompt, mod_s[5], wsg, wsu, wsd, gfin)

    heads = lambda z, b, s: z.reshape(1, b, s, n_heads, dh)
    pool_p = u_p[:, seq - (POOL_HALO - 1):][None]
    pool_s = ext_s[:, ext_s.shape[1] - (POOL_HALO - 1):][None]
    v_p = vt_p.reshape(bsz, n_heads, dh, seq).transpose(0, 3, 1, 2)[None]
    return (y_p, y_s.reshape(db, t_new, d), heads(k_p, bsz, seq), v_p, logf_p[None], pool_p,
            heads(k_s, db, t_new), heads(v_s, db, t_new), logf_s[None], pool_s)
```

```python
import functools

import jax
import jax.numpy as jnp
from jax import lax
from jax.experimental import pallas as pl
from jax.experimental.pallas import tpu as pltpu

F32 = jnp.float32
BF16 = jnp.bfloat16

RMS_EPS = 1e-6
LOG2E = 1.4426950408889634
TOP_K = 8
N_GROUPS = 8
TOPK_GROUPS = 4
ROUTED_SCALE = 2.5
POOL_WINDOWS = (2, 4, 8, 16)
POOL_HALO = 16
LANES = 128
MOE_ROWS = 128
MOE_SUBBLOCKS = 4
MOE_BLOCK = MOE_ROWS * MOE_SUBBLOCKS
VMEM_LIMIT = 56 * 1024 * 1024

_dot = functools.partial(jnp.dot, preferred_element_type=F32)


def _dot_nt(a, b, precision=None):
    return lax.dot_general(a, b, (((1,), (1,)), ((), ())), preferred_element_type=F32, precision=precision)


def _params(*sem):
    return pltpu.CompilerParams(dimension_semantics=sem, vmem_limit_bytes=VMEM_LIMIT)


def _vmem():
    return pl.BlockSpec(memory_space=pltpu.VMEM)


def _rms(x, g):
    return x * lax.rsqrt(jnp.mean(x * x, axis=-1, keepdims=True) + RMS_EPS) * g


def _rms_mod(x, g, shift, scale):
    return _rms(x, g) * (1.0 + scale) + shift


def _log_sigmoid(x):
    return jnp.minimum(x, 0.0) - jnp.log1p(jnp.exp(-jnp.abs(x)))


def _silu(x):
    return x * jax.nn.sigmoid(x)


def _cumsum_lanes(x):
    n = x.shape[-1]
    lane = lax.broadcasted_iota(jnp.int32, x.shape, x.ndim - 1)
    k = 1
    while k < n:
        x = x + jnp.where(lane >= k, pltpu.roll(x, k, axis=x.ndim - 1), 0.0)
        k *= 2
    return x


def _mod_spec(mod, tm, nt):
    _, r, d = mod.shape
    if r == 1:
        return pl.BlockSpec((1, 1, d), lambda b, i: (b, 0, 0))
    return pl.BlockSpec((1, tm, d), lambda b, i: (b, i, 0))


def _ada_kernel(c_ref, w_ref, b_ref, o_ref):
    a = _silu(c_ref[...]).astype(BF16)
    o_ref[...] = _dot(a, w_ref[...].astype(BF16)) + b_ref[...]


def _ada(c, w, b):
    r, d = c.shape
    n = w.shape[1]
    tn = min(n, 1024)
    return pl.pallas_call(
        _ada_kernel,
        grid=(n // tn,),
        in_specs=[pl.BlockSpec((r, d), lambda j: (0, 0)),
                  pl.BlockSpec((d, tn), lambda j: (0, j)),
                  pl.BlockSpec((1, tn), lambda j: (0, j))],
        out_specs=pl.BlockSpec((r, tn), lambda j: (0, j)),
        out_shape=jax.ShapeDtypeStruct((r, n), F32),
        compiler_params=_params("parallel"),
        name="ada",
    )(c, w, b.reshape(1, n))


def _inproj_kernel(x_ref, g_ref, sh_ref, sc_ref, wqt_ref, wk_ref, wvt_ref, wu_ref, wf_ref, bf_ref,
                   qt_ref, k_ref, vt_ref, u_ref, lf_ref, *, qscale):
    h = _rms_mod(x_ref[0], g_ref[...], sh_ref[0], sc_ref[0]).astype(BF16)
    qt_ref[0] = (_dot_nt(wqt_ref[...], h) * qscale).astype(BF16)
    vt_ref[0] = _dot_nt(wvt_ref[...], h)
    k_ref[0] = _dot(h, wk_ref[...])
    u_ref[0] = _dot(h, wu_ref[...])
    lf_ref[0] = _log_sigmoid(_dot(h, wf_ref[...]) + bf_ref[...])


def _inproj(x, g, shift, scale, wq_t, wk, wv_t, wu, wf, bf, qscale):
    bsz, s, d = x.shape
    a, p = wk.shape[1], wu.shape[1]
    tm = min(s, 512)
    nt = s // tm
    row = lambda n: pl.BlockSpec((1, tm, n), lambda b, i: (b, i, 0))
    col = pl.BlockSpec((1, a, tm), lambda b, i: (b, 0, i))
    return pl.pallas_call(
        functools.partial(_inproj_kernel, qscale=qscale),
        grid=(bsz, nt),
        in_specs=[row(d), pl.BlockSpec((1, d), lambda b, i: (0, 0)),
                  _mod_spec(shift, tm, nt), _mod_spec(scale, tm, nt),
                  _vmem(), _vmem(), _vmem(), _vmem(), _vmem(), _vmem()],
        out_specs=[col, row(a), col, row(p), row(LANES)],
        out_shape=[jax.ShapeDtypeStruct((bsz, a, s), BF16),
                   jax.ShapeDtypeStruct((bsz, s, a), F32),
                   jax.ShapeDtypeStruct((bsz, a, s), F32),
                   jax.ShapeDtypeStruct((bsz, s, p), F32),
                   jax.ShapeDtypeStruct((bsz, s, LANES), F32)],
        compiler_params=_params("parallel", "parallel"),
        name="inproj",
    )(x, g, shift, scale, wq_t, wk, wv_t, wu, wf, bf)


def _cumsum_kernel(x_ref, o_ref):
    o_ref[0] = _cumsum_lanes(x_ref[0])


def _cumsum_rows(x):
    bsz, h, s = x.shape
    spec = pl.BlockSpec((1, h, s), lambda b: (b, 0, 0))
    return pl.pallas_call(
        _cumsum_kernel, grid=(bsz,), in_specs=[spec], out_specs=spec,
        out_shape=jax.ShapeDtypeStruct(x.shape, F32),
        compiler_params=_params("parallel"), name="logf_cumsum",
    )(x)


def _attn_prompt_kernel(qt_ref, k_ref, vt_ref, c_ref, o_ref, m_ref, l_ref, acc_ref, *, t, dh):
    qi = pl.program_id(2)
    qt = qt_ref[0]
    row = lax.broadcasted_iota(jnp.int32, (2 * dh, 1), 0)
    zero = jnp.zeros_like(qt)
    qh = (jnp.where(row < dh, qt, zero), jnp.where(row < dh, zero, qt))
    m_ref[...] = jnp.full(m_ref.shape, -jnp.inf, F32)
    l_ref[...] = jnp.zeros(l_ref.shape, F32)
    acc_ref[...] = jnp.zeros(acc_ref.shape, F32)

    def tile(ki, diagonal):
        k0 = pl.multiple_of(ki * t, t)
        kb = k_ref[0, pl.ds(k0, t), :].astype(BF16)
        vt = vt_ref[0, :, pl.ds(k0, t)].astype(BF16)
        cb = c_ref[0, 0, ki] * LOG2E
        pv, corr = [], []
        for j in range(2):
            s = _dot(kb, qh[j]) - cb[:, j:j + 1]
            if diagonal:
                key = lax.broadcasted_iota(jnp.int32, (t, t), 0)
                qry = lax.broadcasted_iota(jnp.int32, (t, t), 1)
                s = jnp.where(key <= qry, s, -jnp.inf)
            m_old = m_ref[j]
            m_new = jnp.maximum(m_old, jnp.max(s, axis=0, keepdims=True))
            p = jnp.exp2(s - m_new)
            cj = jnp.exp2(m_old - m_new)
            l_ref[j] = l_ref[j] * cj + jnp.sum(p, axis=0, keepdims=True)
            m_ref[j] = m_new
            pv.append(_dot(vt[j * dh:(j + 1) * dh], p.astype(BF16)))
            corr.append(jnp.broadcast_to(cj, (dh, t)))
        acc_ref[...] = acc_ref[...] * jnp.concatenate(corr, axis=0) + jnp.concatenate(pv, axis=0)

    def body(ki, carry):
        tile(ki, False)
        return carry

    lax.fori_loop(0, qi, body, 0)
    tile(qi, True)
    denom = jnp.concatenate([jnp.broadcast_to(l_ref[j], (dh, t)) for j in range(2)], axis=0)
    o_ref[0] = (acc_ref[...] / denom).T.astype(o_ref.dtype)


def _attn_prompt(qt, k, vt, cum, dh):
    bsz, s, a = k.shape
    hp = a // (2 * dh)
    t = min(s, 512)
    nt = s // t
    cum = cum.reshape(bsz, hp, 2, nt, t).transpose(0, 1, 3, 4, 2)
    return pl.pallas_call(
        functools.partial(_attn_prompt_kernel, t=t, dh=dh),
        grid=(bsz, hp, nt),
        in_specs=[pl.BlockSpec((1, 2 * dh, t), lambda b, h, i: (b, h, i)),
                  pl.BlockSpec((1, s, 2 * dh), lambda b, h, i: (b, 0, h)),
                  pl.BlockSpec((1, 2 * dh, s), lambda b, h, i: (b, h, 0)),
                  pl.BlockSpec((1, 1, nt, t, 2), lambda b, h, i: (b, h, 0, 0, 0))],
        out_specs=pl.BlockSpec((1, t, 2 * dh), lambda b, h, i: (b, i, h)),
        out_shape=jax.ShapeDtypeStruct((bsz, s, a), BF16),
        scratch_shapes=[pltpu.VMEM((2, 1, t), F32), pltpu.VMEM((2, 1, t), F32),
                        pltpu.VMEM((2 * dh, t), F32)],
        compiler_params=_params("parallel", "parallel", "parallel"),
        name="attn_prompt",
    )(qt, k, vt, cum)


def _logf_pages_kernel(x_ref, rin_ref, tot_ref):
    x = x_ref[...]
    n = x.shape[1]
    lane = lax.broadcasted_iota(jnp.int32, x.shape, 1)
    sfx = x
    k = 1
    while k < n:
        sfx = sfx + jnp.where(lane + k < n, pltpu.roll(sfx, n - k, axis=1), 0.0)
        k *= 2
    rin_ref[...] = sfx - x
    tot_ref[...] = jnp.broadcast_to(sfx[:, 0:1], x.shape)


def _logf_pages(lf_t):
    n_pool, n_heads, page = lf_t.shape
    pb = 64 if n_pool % 64 == 0 else n_pool
    spec = pl.BlockSpec((pb * n_heads, page), lambda i: (i, 0))
    rin, tot = pl.pallas_call(
        _logf_pages_kernel,
        grid=(n_pool // pb,), in_specs=[spec], out_specs=[spec, spec],
        out_shape=[jax.ShapeDtypeStruct((n_pool * n_heads, page), F32)] * 2,
        compiler_params=_params("parallel"), name="logf_pages",
    )(lf_t.reshape(n_pool * n_heads, page))
    return rin.reshape(lf_t.shape), tot.reshape(lf_t.shape)


def _decode_kernel(pt_ref, qb_ref, kn_ref, vn_ref, lfn_ref, *refs, g_pages, n_new, n_heads, dh):
    k_refs = refs[:g_pages]
    v_refs = refs[g_pages:2 * g_pages]
    rin_refs = refs[2 * g_pages:3 * g_pages]
    tot_refs = refs[3 * g_pages:4 * g_pages]
    o_ref, m_ref, l_ref, acc_ref, carry_ref = refs[4 * g_pages:]
    step = pl.program_id(1)
    qb = qb_ref[0]

    @pl.when(step == 0)
    def _():
        c = _cumsum_lanes(lfn_ref[0])
        tot = c[:, LANES - 1:LANES]
        rnew = (tot - c)[:, :16]
        s = _dot_nt(qb, kn_ref[0]) + jnp.concatenate([rnew] * n_new, axis=0)
        tok = lax.broadcasted_iota(jnp.int32, s.shape, 0) // n_heads
        key = lax.broadcasted_iota(jnp.int32, s.shape, 1)
        s = jnp.where(key <= tok, s, -jnp.inf)
        m = jnp.max(s, axis=-1, keepdims=True)
        p = jnp.exp(s - m)
        m_ref[...] = m
        l_ref[...] = jnp.sum(p, axis=-1, keepdims=True)
        acc_ref[...] = _dot(p.astype(BF16), vn_ref[0])
        carry_ref[...] = jnp.broadcast_to(tot, carry_ref.shape)

    carry = carry_ref[...]
    scores = []
    for g in range(g_pages):
        r = carry + rin_refs[g][0]
        carry = carry + tot_refs[g][0]
        kb = k_refs[g][0].astype(BF16)
        scores.append(_dot(qb, kb) + jnp.concatenate([r] * n_new, axis=0))
    carry_ref[...] = carry
    s = jnp.concatenate(scores, axis=1)
    m_old = m_ref[...]
    m_new = jnp.maximum(m_old, jnp.max(s, axis=-1, keepdims=True))
    p = jnp.exp(s - m_new)
    corr = jnp.exp(m_old - m_new)
    l_ref[...] = l_ref[...] * corr + jnp.sum(p, axis=-1, keepdims=True)
    m_ref[...] = m_new
    pb = p.astype(BF16)
    page = s.shape[1] // g_pages
    pv = _dot_nt(pb[:, :page], v_refs[0][0].astype(BF16))
    for g in range(1, g_pages):
        pv = pv + _dot_nt(pb[:, g * page:(g + 1) * page], v_refs[g][0].astype(BF16))
    acc_ref[...] = acc_ref[...] * corr + pv

    @pl.when(step == pl.num_programs(1) - 1)
    def _():
        o = acc_ref[...] / l_ref[...]
        head = lax.broadcasted_iota(jnp.int32, o.shape, 0) % n_heads
        lane_head = lax.broadcasted_iota(jnp.int32, o.shape, 1) // dh
        o = jnp.where(head == lane_head, o, 0.0)
        o_ref[0] = jnp.sum(o.reshape(n_new, n_heads, o.shape[1]), axis=1)


def _decode_attn(q, k, v, logf, cache_kt, cache_vt, cache_logf_t, page_table, n_heads, dh):
    db, t, a = q.shape
    n_pages = page_table.shape[1]
    page = cache_kt.shape[2]
    assert page == LANES and t <= 16
    g_pages = max(g for g in (8, 4, 2, 1) if n_pages % g == 0)
    rows = t * n_heads
    rin, tot = _logf_pages(cache_logf_t)
    eye = jnp.eye(n_heads, dtype=BF16)
    qb = (q.reshape(db, t, 1, n_heads, dh) * eye[None, None, :, :, None]).reshape(db, rows, a)
    pad = lambda z: jnp.pad(z.astype(BF16), ((0, 0), (0, 16 - t), (0, 0)))
    lfn = jnp.pad(logf.transpose(0, 2, 1), ((0, 0), (0, 0), (0, LANES - t)))

    def page_idx(g):
        return lambda b, s, pt: (pt[b * n_pages + n_pages - 1 - (s * g_pages + g)], 0, 0)

    seq = lambda shape: pl.BlockSpec((1,) + shape, lambda b, s, pt: (b, 0, 0))
    in_specs = [seq((rows, a)), seq((16, a)), seq((16, a)), seq((n_heads, LANES))]
    for shape in ((1, a, page), (1, a, page), (1, n_heads, page), (1, n_heads, page)):
        in_specs += [pl.BlockSpec(shape, page_idx(g)) for g in range(g_pages)]
    grid_spec = pltpu.PrefetchScalarGridSpec(
        num_scalar_prefetch=1, grid=(db, n_pages // g_pages), in_specs=in_specs,
        out_specs=seq((t, a)),
        scratch_shapes=[pltpu.VMEM((rows, 1), F32), pltpu.VMEM((rows, 1), F32),
                        pltpu.VMEM((rows, a), F32), pltpu.VMEM((n_heads, page), F32)])
    return pl.pallas_call(
        functools.partial(_decode_kernel, g_pages=g_pages, n_new=t, n_heads=n_heads, dh=dh),
        grid_spec=grid_spec,
        out_shape=jax.ShapeDtypeStruct((db, t, a), F32),
        compiler_params=_params("parallel", "arbitrary"),
        name="attn_decode",
    )(page_table.reshape(-1), qb, pad(k), pad(v), lfn,
      *([cache_kt] * g_pages), *([cache_vt] * g_pages), *([rin] * g_pages), *([tot] * g_pages))


def _pool_prompt_kernel(u_ref, halo_ref, o_ref, *, tm, pg):
    i = pl.program_id(1)
    u = u_ref[0]
    halo = jnp.where(i > 0, halo_ref[0], 0.0)
    ext = jnp.concatenate([halo, u], axis=0)
    pos = i * tm + lax.broadcasted_iota(jnp.int32, (tm, 1), 0)
    outs = []
    for g, w in enumerate(POOL_WINDOWS):
        r = ext[:, g * pg:(g + 1) * pg]
        k = 1
        while k < w:
            r = r + pltpu.roll(r, k, axis=0)
            k *= 2
        count = jnp.minimum(w, pos + 1).astype(F32)
        outs.append(r[POOL_HALO:] / count - u[:, g * pg:(g + 1) * pg])
    o_ref[0] = jnp.concatenate(outs, axis=-1).astype(o_ref.dtype)


def _pool_prompt(u):
    bsz, s, p = u.shape
    tm = min(s, 512)
    hb = tm // POOL_HALO
    return pl.pallas_call(
        functools.partial(_pool_prompt_kernel, tm=tm, pg=p // len(POOL_WINDOWS)),
        grid=(bsz, s // tm),
        in_specs=[pl.BlockSpec((1, tm, p), lambda b, i: (b, i, 0)),
                  pl.BlockSpec((1, POOL_HALO, p), lambda b, i: (b, jnp.maximum(i * hb - 1, 0), 0))],
        out_specs=pl.BlockSpec((1, tm, p), lambda b, i: (b, i, 0)),
        out_shape=jax.ShapeDtypeStruct(u.shape, BF16),
        compiler_params=_params("parallel", "parallel"),
        name="pool_prompt",
    )(u, u)


def _pool_sample_kernel(e_ref, o_ref, *, n_new, n_prev, pg):
    for t in range(n_new):
        e = n_prev + t
        outs = []
        for g, w in enumerate(POOL_WINDOWS):
            cols = slice(g * pg, (g + 1) * pg)
            lo = max(0, e - w + 1)
            acc = e_ref[lo, :, cols]
            for j in range(lo + 1, e + 1):
                acc = acc + e_ref[j, :, cols]
            outs.append(acc / float(min(w, e + 1)) - e_ref[e, :, cols])
        o_ref[t] = jnp.concatenate(outs, axis=-1).astype(o_ref.dtype)


def _pool_sample(ext_t, n_new):
    n_all, db, p = ext_t.shape
    return pl.pallas_call(
        functools.partial(_pool_sample_kernel, n_new=n_new, n_prev=n_all - n_new, pg=p // len(POOL_WINDOWS)),
        in_specs=[_vmem()], out_specs=_vmem(),
        out_shape=jax.ShapeDtypeStruct((n_new, db, p), BF16),
        compiler_params=pltpu.CompilerParams(vmem_limit_bytes=VMEM_LIMIT),
        name="pool_sample",
    )(ext_t)


def _merge_kernel(x_ref, g_ref, sh_ref, sc_ref, gt_ref, oa_ref, pl_ref, wpool_ref, ps_ref,
                  wpa_ref, wpb_ref, wga_ref, wgb_ref, wo_ref, o_ref):
    x = x_ref[0]
    h = _rms_mod(x, g_ref[...], sh_ref[0], sc_ref[0]).astype(BF16)
    pooled = pl_ref[0]
    n_groups, pg, _ = wpool_ref.shape
    mixed = jnp.concatenate(
        [_dot(pooled[:, g * pg:(g + 1) * pg], wpool_ref[g]) for g in range(n_groups)], axis=-1)
    o_pool = (mixed * ps_ref[...]).astype(BF16)
    y = jax.nn.sigmoid(_dot(h, wga_ref[...])) * _dot(oa_ref[0], wpa_ref[...])
    y = y + jax.nn.sigmoid(_dot(h, wgb_ref[...])) * _dot(o_pool, wpb_ref[...])
    o_ref[0] = x + gt_ref[0] * _dot(y.astype(BF16), wo_ref[...])


def _merge(x, g, shift, scale, gate, o_att, pooled, w_pool, pool_scale, w_pa, w_pb, w_ga, w_gb, w_o):
    bsz, s, d = x.shape
    tm = min(s, 256)
    nt = s // tm
    row = lambda n: pl.BlockSpec((1, tm, n), lambda b, i: (b, i, 0))
    return pl.pallas_call(
        _merge_kernel,
        grid=(bsz, nt),
        in_specs=[row(d), pl.BlockSpec((1, d), lambda b, i: (0, 0)),
                  _mod_spec(shift, tm, nt), _mod_spec(scale, tm, nt), _mod_spec(gate, tm, nt),
                  row(o_att.shape[2]), row(pooled.shape[2]),
                  _vmem(), _vmem(), _vmem(), _vmem(), _vmem(), _vmem(), _vmem()],
        out_specs=row(d),
        out_shape=jax.ShapeDtypeStruct(x.shape, F32),
        compiler_params=_params("parallel", "parallel"),
        name="merge",
    )(x, g, shift, scale, gate, o_att, pooled, w_pool, pool_scale, w_pa, w_pb, w_ga, w_gb, w_o)


def _route_kernel(x_ref, g_ref, sh_ref, sc_ref, wr_ref, rb_ref, h_ref, idx_ref, w_ref):
    h = _rms_mod(x_ref[0], g_ref[...], sh_ref[0], sc_ref[0])
    h_ref[0] = h.astype(BF16)
    n_exp = wr_ref.shape[0]
    tm = h.shape[0]
    per = n_exp // N_GROUPS
    scores = jax.nn.sigmoid(_dot_nt(wr_ref[...], h, precision=lax.Precision.HIGHEST))
    sel = scores + rb_ref[...]
    neg = -jnp.inf
    groups = [sel[g * per:(g + 1) * per] for g in range(N_GROUPS)]
    eidx = lax.broadcasted_iota(jnp.int32, (per, tm), 0)
    gscore = []
    for grp in groups:
        m1 = jnp.max(grp, axis=0, keepdims=True)
        first = jnp.min(jnp.where(grp == m1, eidx, per), axis=0, keepdims=True)
        gscore.append(m1 + jnp.max(jnp.where(eidx == first, neg, grp), axis=0, keepdims=True))
    cand = []
    for g in range(N_GROUPS):
        rank = jnp.zeros((1, tm), jnp.int32)
        for o in range(N_GROUPS):
            if o != g:
                ahead = gscore[o] >= gscore[g] if o < g else gscore[o] > gscore[g]
                rank = rank + ahead.astype(jnp.int32)
        cand.append(jnp.where(rank < TOPK_GROUPS, groups[g], neg))
    cand = jnp.concatenate(cand, axis=0)
    row = lax.broadcasted_iota(jnp.int32, (n_exp, tm), 0)
    idxs, wts = [], []
    for _ in range(TOP_K):
        m = jnp.max(cand, axis=0, keepdims=True)
        pick = jnp.min(jnp.where(cand == m, row, n_exp), axis=0, keepdims=True)
        hit = row == pick
        idxs.append(pick)
        wts.append(jnp.sum(jnp.where(hit, scores, 0.0), axis=0, keepdims=True))
        cand = jnp.where(hit, neg, cand)
    wts = jnp.concatenate(wts, axis=0)
    idx_ref[...] = jnp.concatenate(idxs, axis=0)
    w_ref[...] = wts / jnp.sum(wts, axis=0, keepdims=True) * ROUTED_SCALE


def _route(x, g, shift, scale, w_router_t, router_bias):
    bsz, s, d = x.shape
    tm = min(s, 256)
    nt = s // tm
    col = pl.BlockSpec((TOP_K, tm), lambda b, i: (0, b * nt + i))
    return pl.pallas_call(
        _route_kernel,
        grid=(bsz, nt),
        in_specs=[pl.BlockSpec((1, tm, d), lambda b, i: (b, i, 0)), pl.BlockSpec((1, d), lambda b, i: (0, 0)),
                  _mod_spec(shift, tm, nt), _mod_spec(scale, tm, nt), _vmem(), _vmem()],
        out_specs=[pl.BlockSpec((1, tm, d), lambda b, i: (b, i, 0)), col, col],
        out_shape=[jax.ShapeDtypeStruct(x.shape, BF16),
                   jax.ShapeDtypeStruct((TOP_K, bsz * s), jnp.int32),
                   jax.ShapeDtypeStruct((TOP_K, bsz * s), F32)],
        compiler_params=_params("parallel", "parallel"),
        name="route",
    )(x, g, shift, scale, w_router_t, router_bias)


def _moe_kernel(blk_ref, exp_ref, lo_ref, hi_ref, lead_ref, nxt_ref, par_ref, xs_ref, wt_ref,
                wg_hbm, wu_hbm, wd_hbm, o_ref, acc_ref, wg_f, wu_f, wd_f, wg_b, wu_b, wd_b, sems):
    i = pl.program_id(0)
    lo = lo_ref[i]
    hi = hi_ref[i]

    def fetch(expert, slot):
        return [pltpu.make_async_copy(src.at[expert], dst.at[slot], sems.at[n, slot])
                for n, (src, dst) in enumerate(((wg_hbm, wg_f), (wu_hbm, wu_f), (wd_hbm, wd_f)))]

    @pl.when(i == 0)
    def _():
        for c in fetch(exp_ref[0], 0):
            c.start()

    @pl.when(lead_ref[i] == 1)
    def _():
        slot = par_ref[i]
        nxt = nxt_ref[i]

        @pl.when(nxt >= 0)
        def _():
            for c in fetch(nxt, 1 - slot):
                c.start()

        for c in fetch(exp_ref[i], slot):
            c.wait()
        wg_b[...] = wg_f[slot].astype(BF16)
        wu_b[...] = wu_f[slot].astype(BF16)
        wd_b[...] = wd_f[slot].astype(BF16)

    for j in range(MOE_SUBBLOCKS):
        r0 = j * MOE_ROWS

        @pl.when((lo < r0 + MOE_ROWS) & (hi > r0))
        def _(j=j, r0=r0):
            rows = pl.ds(r0, MOE_ROWS)
            xb = xs_ref[rows, :]
            row = r0 + lax.broadcasted_iota(jnp.int32, (MOE_ROWS, 1), 0)
            wt = jnp.where((row >= lo) & (row < hi), wt_ref[rows, :], 0.0)
            hid = (_silu(_dot(xb, wg_b[...])) * _dot(xb, wu_b[...])).astype(BF16)
            y = _dot(hid, wd_b[...]) * wt

            @pl.when(lo <= r0)
            def _():
                acc_ref[j] = y

            @pl.when(lo > r0)
            def _():
                acc_ref[j] = acc_ref[j] + y

            o_ref[rows, :] = acc_ref[j].astype(o_ref.dtype)


def _moe(xs, row_w, e_sorted, start, w_gate, w_up, w_down):
    n_assign, d = xs.shape
    n_exp, _, f = w_gate.shape
    n_blocks = -(-n_assign // MOE_BLOCK)
    cuts = jnp.sort(jnp.concatenate([jnp.arange(n_blocks, dtype=jnp.int32) * MOE_BLOCK, start[1:]]))
    n_items = cuts.shape[0]
    ends = jnp.concatenate([cuts[1:], jnp.full((1,), n_assign, jnp.int32)])
    blk = jnp.minimum(cuts // MOE_BLOCK, n_blocks - 1)
    expert = e_sorted[jnp.minimum(cuts, n_assign - 1)]
    lo, hi = cuts - blk * MOE_BLOCK, ends - blk * MOE_BLOCK
    live = ends > cuts
    item = jnp.arange(n_items, dtype=jnp.int32)
    change = jnp.concatenate([jnp.ones((1,), bool), expert[1:] != expert[:-1]])
    seen = jnp.cumsum(live.astype(jnp.int32))
    seen_at_run_start = lax.cummax(jnp.where(change, seen - live, 0))
    lead = live & (seen - seen_at_run_start == 1)
    parity = (jnp.cumsum(lead.astype(jnp.int32)) - 1) % 2
    lead_at = jnp.where(lead, item, n_items)
    nxt_item = lax.cummin(jnp.concatenate([lead_at[1:], jnp.full((1,), n_items, jnp.int32)]), reverse=True)
    nxt = jnp.where(nxt_item < n_items, expert[jnp.minimum(nxt_item, n_items - 1)], -1)
    row = lambda n: pl.BlockSpec((MOE_BLOCK, n), lambda i, blk, *_: (blk[i], 0))
    hbm = pl.BlockSpec(memory_space=pl.ANY)
    grid_spec = pltpu.PrefetchScalarGridSpec(
        num_scalar_prefetch=7, grid=(n_items,),
        in_specs=[row(d), row(1), hbm, hbm, hbm],
        out_specs=row(d),
        scratch_shapes=[pltpu.VMEM((MOE_SUBBLOCKS, MOE_ROWS, d), F32),
                        pltpu.VMEM((2, d, f), F32), pltpu.VMEM((2, d, f), F32), pltpu.VMEM((2, f, d), F32),
                        pltpu.VMEM((d, f), BF16), pltpu.VMEM((d, f), BF16), pltpu.VMEM((f, d), BF16),
                        pltpu.SemaphoreType.DMA((3, 2))])
    i32 = lambda z: z.astype(jnp.int32)
    return pl.pallas_call(
        _moe_kernel, grid_spec=grid_spec,
        out_shape=jax.ShapeDtypeStruct((n_assign, d), BF16),
        compiler_params=_params("arbitrary"),
        name="moe",
    )(i32(blk), i32(expert), i32(lo), i32(hi), i32(lead), i32(nxt), i32(parity),
      xs, row_w, w_gate, w_up, w_down)


def _final_kernel(x_ref, h_ref, r_ref, gt_ref, wg_ref, wu_ref, wd_ref, gf_ref, o_ref):
    hb = h_ref[0]
    hid = (_silu(_dot(hb, wg_ref[...])) * _dot(hb, wu_ref[...])).astype(BF16)
    moe = _dot(hid, wd_ref[...])
    for k in range(r_ref.shape[0]):
        moe = moe + r_ref[k].astype(F32)
    o_ref[0] = _rms(x_ref[0] + gt_ref[0] * moe, gf_ref[...])


def _final(x, h, routed, row0, gate, w_sg, w_su, w_sd, g_final):
    bsz, s, d = x.shape
    tm = min(s, 256)
    nt = s // tm
    assert row0 % tm == 0
    rb0 = row0 // tm
    row = pl.BlockSpec((1, tm, d), lambda b, i: (b, i, 0))
    return pl.pallas_call(
        _final_kernel,
        grid=(bsz, nt),
        in_specs=[row, row, pl.BlockSpec((routed.shape[0], tm, d), lambda b, i: (0, rb0 + b * nt + i, 0)),
                  _mod_spec(gate, tm, nt), _vmem(), _vmem(), _vmem(), pl.BlockSpec((1, d), lambda b, i: (0, 0))],
        out_specs=row,
        out_shape=jax.ShapeDtypeStruct(x.shape, F32),
        compiler_params=_params("parallel", "parallel"),
        name="final",
    )(x, h, routed, gate, w_sg, w_su, w_sd, g_final)


def kernel(x_prompt, x_sample, cache_k, cache_v, cache_logf, state_pool, page_table, c_prompt, c_sample, w_ada, b_ada, g_mix, w_in, b_f, w_pool, pool_scale, w_pa, w_pb, w_o, g_ffn, w_router, router_bias, w_gate, w_up, w_down, w_sh_gate, w_sh_up, w_sh_down, g_final):
    depth = w_ada.shape[0]
    assert depth == 1
    bsz, seq, d = x_prompt.shape
    db, t_new, _ = x_sample.shape
    _, n_pool_pages, page, n_heads, dh = cache_k.shape
    a = n_heads * dh
    p = state_pool.shape[3]
    n_exp = w_router.shape[2]
    n_dec = db * t_new

    w = w_in[0]
    cuts = [a, 2 * a, 3 * a, 3 * a + n_heads, 3 * a + n_heads + p, 3 * a + n_heads + p + d]
    wq, wk, wv, wf, wu, wga, wgb = [z.astype(BF16) for z in jnp.split(w, cuts, axis=1)]
    wq_t, wv_t = wq.T, wv.T
    wf = jnp.pad(wf, ((0, 0), (0, LANES - n_heads)))
    bf = jnp.pad(b_f[0], (0, LANES - n_heads)).reshape(1, LANES)
    wpool, wpa, wpb, wo = w_pool[0].astype(BF16), w_pa[0].astype(BF16), w_pb[0].astype(BF16), w_o[0].astype(BF16)
    wsg, wsu, wsd = w_sh_gate[0].astype(BF16), w_sh_up[0].astype(BF16), w_sh_down[0].astype(BF16)
    wr_t = w_router[0].T
    rbias = router_bias[0].reshape(n_exp, 1)
    gmix, gffn, gfin = g_mix[0].reshape(1, d), g_ffn[0].reshape(1, d), g_final.reshape(1, d)
    pscale = pool_scale[0].reshape(1, p)

    c_all = jnp.concatenate([c_prompt, c_sample], axis=0)
    r_pad = -c_all.shape[0] % 16
    mod = _ada(jnp.pad(c_all, ((0, r_pad), (0, 0))), w_ada[0], b_ada[0])
    mod_p = [m.reshape(bsz, 1, d) for m in jnp.split(mod[:bsz], 6, axis=-1)]
    mod_s = [jnp.repeat(m, t_new, axis=0).reshape(1, n_dec, d) for m in jnp.split(mod[bsz:bsz + db], 6, axis=-1)]

    def mixer_in(x, m, qscale):
        return _inproj(x, gmix, m[0], m[1], wq_t, wk, wv_t, wu, wf, bf, qscale)

    def mixer_out(x, m, o_att, pooled):
        x1 = _merge(x, gmix, m[0], m[1], m[2], o_att, pooled, wpool, pscale, wpa, wpb, wga, wgb, wo)
        return (x1,) + tuple(_route(x1, gffn, m[3], m[4], wr_t, rbias))

    qt_p, k_p, vt_p, u_p, lf_p = mixer_in(x_prompt, mod_p, dh ** -0.5 * LOG2E)
    logf_p = lf_p[:, :, :n_heads]
    cum_p = _cumsum_rows(logf_p.transpose(0, 2, 1))
    o_att_p = _attn_prompt(qt_p, k_p, vt_p, cum_p, dh)
    x1_p, h2_p, idx_p, wt_p = mixer_out(x_prompt, mod_p, o_att_p, _pool_prompt(u_p))

    xs3 = x_sample.reshape(1, n_dec, d)
    qt_s, k_s, vt_s, u_s, lf_s = mixer_in(xs3, mod_s, dh ** -0.5)
    q_s, v_s = qt_s[0].T, vt_s[0].T
    logf_s = lf_s[0, :, :n_heads].reshape(db, t_new, n_heads)
    u_s = u_s.reshape(db, t_new, p)
    cache_kt = cache_k[0].transpose(0, 2, 3, 1).reshape(n_pool_pages, a, page)
    cache_vt = cache_v[0].transpose(0, 2, 3, 1).reshape(n_pool_pages, a, page)
    o_att_s = _decode_attn(q_s.reshape(db, t_new, a), k_s.reshape(db, t_new, a), v_s.reshape(db, t_new, a),
                           logf_s, cache_kt, cache_vt, cache_logf[0].transpose(0, 2, 1), page_table, n_heads, dh)
    ext_s = jnp.concatenate([state_pool[0], u_s], axis=1)
    pooled_s = _pool_sample(ext_s.transpose(1, 0, 2), t_new).transpose(1, 0, 2).reshape(1, n_dec, p)
    x1_s, h2_s, idx_s, wt_s = mixer_out(xs3, mod_s, o_att_s.astype(BF16).reshape(1, n_dec, a), pooled_s)

    n_prompt = bsz * seq
    n_tok = n_prompt + n_dec
    h2_all = jnp.concatenate([h2_p.reshape(n_prompt, d), h2_s.reshape(n_dec, d)], axis=0)
    flat_e = jnp.concatenate([idx_p, idx_s], axis=1).T.reshape(-1)
    flat_w = jnp.concatenate([wt_p, wt_s], axis=1).T.reshape(-1)
    n_assign = n_tok * TOP_K
    ids = jnp.arange(n_assign, dtype=jnp.int32)
    e_sorted, order, w_sorted = lax.sort((flat_e, ids, flat_w), num_keys=1, is_stable=True)
    _, rank = lax.sort((order, ids), num_keys=1)
    start = jnp.searchsorted(e_sorted, jnp.arange(n_exp, dtype=jnp.int32), side='left').astype(jnp.int32)
    xs = h2_all[order // TOP_K]
    ys = _moe(xs, w_sorted.reshape(n_assign, 1), e_sorted, start, w_gate[0], w_up[0], w_down[0])
    routed = ys[rank.reshape(n_tok, TOP_K).T.reshape(-1)].reshape(TOP_K, n_tok, d)

    y_p = _final(x1_p, h2_p, routed, 0, mod_p[5], wsg, wsu, wsd, gfin)
    y_s = _final(x1_s, h2_s, routed, n_prompt, mod_s[5], wsg, wsu, wsd, gfin)

    heads = lambda z, b, s: z.reshape(1, b, s, n_heads, dh)
    pool_p = u_p[:, seq - (POOL_HALO - 1):][None]
    pool_s = ext_s[:, ext_s.shape[1] - (POOL_HALO - 1):][None]
    v_p = vt_p.reshape(bsz, n_heads, dh, seq).transpose(0, 3, 1, 2)[None]
    return (y_p, y_s.reshape(db, t_new, d), heads(k_p, bsz, seq), v_p, logf_p[None], pool_p,
            heads(k_s, db, t_new), heads(v_s, db, t_new), logf_s[None], pool_s)
```

```python
import functools

import jax
import jax.numpy as jnp
from jax import lax
from jax.experimental import pallas as pl
from jax.experimental.pallas import tpu as pltpu

F32 = jnp.float32
BF16 = jnp.bfloat16

RMS_EPS = 1e-6
LOG2E = 1.4426950408889634
TOP_K = 8
N_GROUPS = 8
TOPK_GROUPS = 4
ROUTED_SCALE = 2.5
POOL_WINDOWS = (2, 4, 8, 16)
POOL_HALO = 16
LANES = 128
DECODE_SLOTS = 3
MOE_ROWS = 128
MOE_SUBBLOCKS = 4
MOE_BLOCK = MOE_ROWS * MOE_SUBBLOCKS
VMEM_LIMIT = 56 * 1024 * 1024

_dot = functools.partial(jnp.dot, preferred_element_type=F32)


def _dot_nt(a, b, precision=None):
    return lax.dot_general(a, b, (((1,), (1,)), ((), ())), preferred_element_type=F32, precision=precision)


def _params(*sem):
    return pltpu.CompilerParams(dimension_semantics=sem, vmem_limit_bytes=VMEM_LIMIT)


def _vmem():
    return pl.BlockSpec(memory_space=pltpu.VMEM)


def _rms(x, g):
    return x * lax.rsqrt(jnp.mean(x * x, axis=-1, keepdims=True) + RMS_EPS) * g


def _rms_mod(x, g, shift, scale):
    return _rms(x, g) * (1.0 + scale) + shift


def _log_sigmoid(x):
    return jnp.minimum(x, 0.0) - jnp.log1p(jnp.exp(-jnp.abs(x)))


def _silu(x):
    return x * jax.nn.sigmoid(x)


def _cumsum_lanes(x):
    n = x.shape[-1]
    lane = lax.broadcasted_iota(jnp.int32, x.shape, x.ndim - 1)
    k = 1
    while k < n:
        x = x + jnp.where(lane >= k, pltpu.roll(x, k, axis=x.ndim - 1), 0.0)
        k *= 2
    return x


def _mod_spec(mod, tm, nt):
    _, r, d = mod.shape
    if r == 1:
        return pl.BlockSpec((1, 1, d), lambda b, i: (b, 0, 0))
    return pl.BlockSpec((1, tm, d), lambda b, i: (b, i, 0))


def _ada_kernel(c_ref, w_ref, b_ref, o_ref):
    a = _silu(c_ref[...]).astype(BF16)
    o_ref[...] = _dot(a, w_ref[...].astype(BF16)) + b_ref[...]


def _ada(c, w, b):
    r, d = c.shape
    n = w.shape[1]
    tn = min(n, 1024)
    return pl.pallas_call(
        _ada_kernel,
        grid=(n // tn,),
        in_specs=[pl.BlockSpec((r, d), lambda j: (0, 0)),
                  pl.BlockSpec((d, tn), lambda j: (0, j)),
                  pl.BlockSpec((1, tn), lambda j: (0, j))],
        out_specs=pl.BlockSpec((r, tn), lambda j: (0, j)),
        out_shape=jax.ShapeDtypeStruct((r, n), F32),
        compiler_params=_params("parallel"),
        name="ada",
    )(c, w, b.reshape(1, n))


def _inproj_kernel(x_ref, g_ref, sh_ref, sc_ref, wqt_ref, wk_ref, wvt_ref, wu_ref, wf_ref, bf_ref,
                   qt_ref, k_ref, vt_ref, u_ref, lf_ref, *, qscale):
    h = _rms_mod(x_ref[0], g_ref[...], sh_ref[0], sc_ref[0]).astype(BF16)
    qt_ref[0] = (_dot_nt(wqt_ref[...], h) * qscale).astype(BF16)
    vt_ref[0] = _dot_nt(wvt_ref[...], h)
    k_ref[0] = _dot(h, wk_ref[...])
    u_ref[0] = _dot(h, wu_ref[...])
    lf_ref[0] = _log_sigmoid(_dot(h, wf_ref[...]) + bf_ref[...])


def _inproj(x, g, shift, scale, wq_t, wk, wv_t, wu, wf, bf, qscale):
    bsz, s, d = x.shape
    a, p = wk.shape[1], wu.shape[1]
    tm = min(s, 512)
    nt = s // tm
    row = lambda n: pl.BlockSpec((1, tm, n), lambda b, i: (b, i, 0))
    col = pl.BlockSpec((1, a, tm), lambda b, i: (b, 0, i))
    return pl.pallas_call(
        functools.partial(_inproj_kernel, qscale=qscale),
        grid=(bsz, nt),
        in_specs=[row(d), pl.BlockSpec((1, d), lambda b, i: (0, 0)),
                  _mod_spec(shift, tm, nt), _mod_spec(scale, tm, nt),
                  _vmem(), _vmem(), _vmem(), _vmem(), _vmem(), _vmem()],
        out_specs=[col, row(a), col, row(p), row(LANES)],
        out_shape=[jax.ShapeDtypeStruct((bsz, a, s), BF16),
                   jax.ShapeDtypeStruct((bsz, s, a), F32),
                   jax.ShapeDtypeStruct((bsz, a, s), F32),
                   jax.ShapeDtypeStruct((bsz, s, p), F32),
                   jax.ShapeDtypeStruct((bsz, s, LANES), F32)],
        compiler_params=_params("parallel", "parallel"),
        name="inproj",
    )(x, g, shift, scale, wq_t, wk, wv_t, wu, wf, bf)


def _cumsum_kernel(x_ref, o_ref):
    o_ref[0] = _cumsum_lanes(x_ref[0])


def _cumsum_rows(x):
    bsz, h, s = x.shape
    spec = pl.BlockSpec((1, h, s), lambda b: (b, 0, 0))
    return pl.pallas_call(
        _cumsum_kernel, grid=(bsz,), in_specs=[spec], out_specs=spec,
        out_shape=jax.ShapeDtypeStruct(x.shape, F32),
        compiler_params=_params("parallel"), name="logf_cumsum",
    )(x)


def _attn_prompt_kernel(qt_ref, k_ref, vt_ref, c_ref, o_ref, m_ref, l_ref, acc_ref, *, t, dh):
    qi = pl.program_id(2)
    qt = qt_ref[0]
    row = lax.broadcasted_iota(jnp.int32, (2 * dh, 1), 0)
    zero = jnp.zeros_like(qt)
    qh = (jnp.where(row < dh, qt, zero), jnp.where(row < dh, zero, qt))
    m_ref[...] = jnp.full(m_ref.shape, -jnp.inf, F32)
    l_ref[...] = jnp.zeros(l_ref.shape, F32)
    acc_ref[...] = jnp.zeros(acc_ref.shape, F32)

    def tile(ki, diagonal):
        k0 = pl.multiple_of(ki * t, t)
        kb = k_ref[0, pl.ds(k0, t), :].astype(BF16)
        vt = vt_ref[0, :, pl.ds(k0, t)].astype(BF16)
        cb = c_ref[0, 0, ki] * LOG2E
        pv, corr = [], []
        for j in range(2):
            s = _dot(kb, qh[j]) - cb[:, j:j + 1]
            if diagonal:
                key = lax.broadcasted_iota(jnp.int32, (t, t), 0)
                qry = lax.broadcasted_iota(jnp.int32, (t, t), 1)
                s = jnp.where(key <= qry, s, -jnp.inf)
            m_old = m_ref[j]
            m_new = jnp.maximum(m_old, jnp.max(s, axis=0, keepdims=True))
            p = jnp.exp2(s - m_new)
            cj = jnp.exp2(m_old - m_new)
            l_ref[j] = l_ref[j] * cj + jnp.sum(p, axis=0, keepdims=True)
            m_ref[j] = m_new
            pv.append(_dot(vt[j * dh:(j + 1) * dh], p.astype(BF16)))
            corr.append(jnp.broadcast_to(cj, (dh, t)))
        acc_ref[...] = acc_ref[...] * jnp.concatenate(corr, axis=0) + jnp.concatenate(pv, axis=0)

    def body(ki, carry):
        tile(ki, False)
        return carry

    lax.fori_loop(0, qi, body, 0)
    tile(qi, True)
    denom = jnp.concatenate([jnp.broadcast_to(l_ref[j], (dh, t)) for j in range(2)], axis=0)
    o_ref[0] = (acc_ref[...] / denom).T.astype(o_ref.dtype)


def _attn_prompt(qt, k, vt, cum, dh):
    bsz, s, a = k.shape
    hp = a // (2 * dh)
    t = min(s, 512)
    nt = s // t
    cum = cum.reshape(bsz, hp, 2, nt, t).transpose(0, 1, 3, 4, 2)
    return pl.pallas_call(
        functools.partial(_attn_prompt_kernel, t=t, dh=dh),
        grid=(bsz, hp, nt),
        in_specs=[pl.BlockSpec((1, 2 * dh, t), lambda b, h, i: (b, h, i)),
                  pl.BlockSpec((1, s, 2 * dh), lambda b, h, i: (b, 0, h)),
                  pl.BlockSpec((1, 2 * dh, s), lambda b, h, i: (b, h, 0)),
                  pl.BlockSpec((1, 1, nt, t, 2), lambda b, h, i: (b, h, 0, 0, 0))],
        out_specs=pl.BlockSpec((1, t, 2 * dh), lambda b, h, i: (b, i, h)),
        out_shape=jax.ShapeDtypeStruct((bsz, s, a), BF16),
        scratch_shapes=[pltpu.VMEM((2, 1, t), F32), pltpu.VMEM((2, 1, t), F32),
                        pltpu.VMEM((2 * dh, t), F32)],
        compiler_params=_params("parallel", "parallel", "parallel"),
        name="attn_prompt",
    )(qt, k, vt, cum)


def _logf_pages_kernel(x_ref, rin_ref, tot_ref):
    x = x_ref[...]
    n = x.shape[1]
    lane = lax.broadcasted_iota(jnp.int32, x.shape, 1)
    sfx = x
    k = 1
    while k < n:
        sfx = sfx + jnp.where(lane + k < n, pltpu.roll(sfx, n - k, axis=1), 0.0)
        k *= 2
    rin_ref[...] = sfx - x
    tot_ref[...] = jnp.broadcast_to(sfx[:, 0:1], x.shape)


def _logf_pages(lf_t):
    n_pool, n_heads, page = lf_t.shape
    pb = 64 if n_pool % 64 == 0 else n_pool
    spec = pl.BlockSpec((pb * n_heads, page), lambda i: (i, 0))
    rin, tot = pl.pallas_call(
        _logf_pages_kernel,
        grid=(n_pool // pb,), in_specs=[spec], out_specs=[spec, spec],
        out_shape=[jax.ShapeDtypeStruct((n_pool * n_heads, page), F32)] * 2,
        compiler_params=_params("parallel"), name="logf_pages",
    )(lf_t.reshape(n_pool * n_heads, page))
    return rin.reshape(lf_t.shape), tot.reshape(lf_t.shape)


def _decode_kernel(pt_ref, qb_ref, kn_ref, vn_ref, lfn_ref, *refs, g_pages, n_pages, n_new, n_heads, dh):
    rin_refs = refs[:g_pages]
    tot_refs = refs[g_pages:2 * g_pages]
    k_hbm, v_hbm, o_ref, m_ref, l_ref, acc_ref, carry_ref, k_buf, v_buf, sems = refs[2 * g_pages:]
    step = pl.program_id(1)
    steps = pl.num_programs(1)
    t = pl.program_id(0) * steps + step
    total = pl.num_programs(0) * steps
    qb = qb_ref[0]

    def fetch(t_of, slot):
        seq, stp = t_of // steps, t_of % steps
        copies = []
        for g in range(g_pages):
            pg = pt_ref[seq * n_pages + n_pages - 1 - (stp * g_pages + g)]
            copies.append(pltpu.make_async_copy(k_hbm.at[pg], k_buf.at[slot, g], sems.at[0, slot]))
            copies.append(pltpu.make_async_copy(v_hbm.at[pg], v_buf.at[slot, g], sems.at[1, slot]))
        return copies

    @pl.when(t == 0)
    def _():
        for ahead in range(DECODE_SLOTS - 1):
            @pl.when(ahead < total)
            def _(ahead=ahead):
                for c in fetch(ahead, ahead):
                    c.start()

    @pl.when(t + DECODE_SLOTS - 1 < total)
    def _():
        for c in fetch(t + DECODE_SLOTS - 1, (t + DECODE_SLOTS - 1) % DECODE_SLOTS):
            c.start()

    @pl.when(step == 0)
    def _():
        c = _cumsum_lanes(lfn_ref[0])
        tot = c[:, LANES - 1:LANES]
        rnew = (tot - c)[:, :16]
        s = _dot_nt(qb, kn_ref[0]) + jnp.concatenate([rnew] * n_new, axis=0)
        tok = lax.broadcasted_iota(jnp.int32, s.shape, 0) // n_heads
        key = lax.broadcasted_iota(jnp.int32, s.shape, 1)
        s = jnp.where(key <= tok, s, -jnp.inf)
        m = jnp.max(s, axis=-1, keepdims=True)
        p = jnp.exp(s - m)
        m_ref[...] = m
        l_ref[...] = jnp.sum(p, axis=-1, keepdims=True)
        acc_ref[...] = _dot(p.astype(BF16), vn_ref[0])
        carry_ref[...] = jnp.broadcast_to(tot, carry_ref.shape)

    slot = t % DECODE_SLOTS
    for c in fetch(t, slot):
        c.wait()
    carry = carry_ref[...]
    scores = []
    for g in range(g_pages):
        r = carry + rin_refs[g][0]
        carry = carry + tot_refs[g][0]
        kb = k_buf[slot, g].astype(BF16)
        scores.append(_dot(qb, kb) + jnp.concatenate([r] * n_new, axis=0))
    carry_ref[...] = carry
    s = jnp.concatenate(scores, axis=1)
    m_old = m_ref[...]
    m_new = jnp.maximum(m_old, jnp.max(s, axis=-1, keepdims=True))
    p = jnp.exp(s - m_new)
    corr = jnp.exp(m_old - m_new)
    l_ref[...] = l_ref[...] * corr + jnp.sum(p, axis=-1, keepdims=True)
    m_ref[...] = m_new
    pb = p.astype(BF16)
    page = s.shape[1] // g_pages
    pv = _dot_nt(pb[:, :page], v_buf[slot, 0].astype(BF16))
    for g in range(1, g_pages):
        pv = pv + _dot_nt(pb[:, g * page:(g + 1) * page], v_buf[slot, g].astype(BF16))
    acc_ref[...] = acc_ref[...] * corr + pv

    @pl.when(step == steps - 1)
    def _():
        o = acc_ref[...] / l_ref[...]
        head = lax.broadcasted_iota(jnp.int32, o.shape, 0) % n_heads
        lane_head = lax.broadcasted_iota(jnp.int32, o.shape, 1) // dh
        o = jnp.where(head == lane_head, o, 0.0)
        o_ref[0] = jnp.sum(o.reshape(n_new, n_heads, o.shape[1]), axis=1)


def _decode_attn(q, k, v, logf, cache_kt, cache_vt, cache_logf_t, page_table, n_heads, dh):
    db, t, a = q.shape
    n_pages = page_table.shape[1]
    page = cache_kt.shape[2]
    assert page == LANES and t <= 16
    g_pages = max(g for g in (8, 4, 2, 1) if n_pages % g == 0)
    rows = t * n_heads
    rin, tot = _logf_pages(cache_logf_t)
    eye = jnp.eye(n_heads, dtype=BF16)
    qb = (q.reshape(db, t, 1, n_heads, dh) * eye[None, None, :, :, None]).reshape(db, rows, a)
    pad = lambda z: jnp.pad(z.astype(BF16), ((0, 0), (0, 16 - t), (0, 0)))
    lfn = jnp.pad(logf.transpose(0, 2, 1), ((0, 0), (0, 0), (0, LANES - t)))

    def page_idx(g):
        return lambda b, s, pt: (pt[b * n_pages + n_pages - 1 - (s * g_pages + g)], 0, 0)

    seq = lambda shape: pl.BlockSpec((1,) + shape, lambda b, s, pt: (b, 0, 0))
    hbm = pl.BlockSpec(memory_space=pl.ANY)
    in_specs = [seq((rows, a)), seq((16, a)), seq((16, a)), seq((n_heads, LANES))]
    for _ in range(2):
        in_specs += [pl.BlockSpec((1, n_heads, page), page_idx(g)) for g in range(g_pages)]
    in_specs += [hbm, hbm]
    grid_spec = pltpu.PrefetchScalarGridSpec(
        num_scalar_prefetch=1, grid=(db, n_pages // g_pages), in_specs=in_specs,
        out_specs=seq((t, a)),
        scratch_shapes=[pltpu.VMEM((rows, 1), F32), pltpu.VMEM((rows, 1), F32),
                        pltpu.VMEM((rows, a), F32), pltpu.VMEM((n_heads, page), F32),
                        pltpu.VMEM((DECODE_SLOTS, g_pages, a, page), F32),
                        pltpu.VMEM((DECODE_SLOTS, g_pages, a, page), F32),
                        pltpu.SemaphoreType.DMA((2, DECODE_SLOTS))])
    return pl.pallas_call(
        functools.partial(_decode_kernel, g_pages=g_pages, n_pages=n_pages, n_new=t, n_heads=n_heads, dh=dh),
        grid_spec=grid_spec,
        out_shape=jax.ShapeDtypeStruct((db, t, a), F32),
        compiler_params=_params("arbitrary", "arbitrary"),
        name="attn_decode",
    )(page_table.reshape(-1), qb, pad(k), pad(v), lfn, *([rin] * g_pages), *([tot] * g_pages), cache_kt, cache_vt)


def _pool_prompt_kernel(u_ref, halo_ref, o_ref, *, tm, pg):
    i = pl.program_id(1)
    u = u_ref[0]
    halo = jnp.where(i > 0, halo_ref[0], 0.0)
    ext = jnp.concatenate([halo, u], axis=0)
    pos = i * tm + lax.broadcasted_iota(jnp.int32, (tm, 1), 0)
    outs = []
    for g, w in enumerate(POOL_WINDOWS):
        r = ext[:, g * pg:(g + 1) * pg]
        k = 1
        while k < w:
            r = r + pltpu.roll(r, k, axis=0)
            k *= 2
        count = jnp.minimum(w, pos + 1).astype(F32)
        outs.append(r[POOL_HALO:] / count - u[:, g * pg:(g + 1) * pg])
    o_ref[0] = jnp.concatenate(outs, axis=-1).astype(o_ref.dtype)


def _pool_prompt(u):
    bsz, s, p = u.shape
    tm = min(s, 512)
    hb = tm // POOL_HALO
    return pl.pallas_call(
        functools.partial(_pool_prompt_kernel, tm=tm, pg=p // len(POOL_WINDOWS)),
        grid=(bsz, s // tm),
        in_specs=[pl.BlockSpec((1, tm, p), lambda b, i: (b, i, 0)),
                  pl.BlockSpec((1, POOL_HALO, p), lambda b, i: (b, jnp.maximum(i * hb - 1, 0), 0))],
        out_specs=pl.BlockSpec((1, tm, p), lambda b, i: (b, i, 0)),
        out_shape=jax.ShapeDtypeStruct(u.shape, BF16),
        compiler_params=_params("parallel", "parallel"),
        name="pool_prompt",
    )(u, u)


def _pool_sample_kernel(e_ref, o_ref, *, n_new, n_prev, pg):
    for t in range(n_new):
        e = n_prev + t
        outs = []
        for g, w in enumerate(POOL_WINDOWS):
            cols = slice(g * pg, (g + 1) * pg)
            lo = max(0, e - w + 1)
            acc = e_ref[lo, :, cols]
            for j in range(lo + 1, e + 1):
                acc = acc + e_ref[j, :, cols]
            outs.append(acc / float(min(w, e + 1)) - e_ref[e, :, cols])
        o_ref[t] = jnp.concatenate(outs, axis=-1).astype(o_ref.dtype)


def _pool_sample(ext_t, n_new):
    n_all, db, p = ext_t.shape
    return pl.pallas_call(
        functools.partial(_pool_sample_kernel, n_new=n_new, n_prev=n_all - n_new, pg=p // len(POOL_WINDOWS)),
        in_specs=[_vmem()], out_specs=_vmem(),
        out_shape=jax.ShapeDtypeStruct((n_new, db, p), BF16),
        compiler_params=pltpu.CompilerParams(vmem_limit_bytes=VMEM_LIMIT),
        name="pool_sample",
    )(ext_t)


def _merge_kernel(x_ref, g_ref, sh_ref, sc_ref, gt_ref, oa_ref, pl_ref, wpool_ref, ps_ref,
                  wpa_ref, wpb_ref, wga_ref, wgb_ref, wo_ref, o_ref):
    x = x_ref[0]
    h = _rms_mod(x, g_ref[...], sh_ref[0], sc_ref[0]).astype(BF16)
    pooled = pl_ref[0]
    n_groups, pg, _ = wpool_ref.shape
    mixed = jnp.concatenate(
        [_dot(pooled[:, g * pg:(g + 1) * pg], wpool_ref[g]) for g in range(n_groups)], axis=-1)
    o_pool = (mixed * ps_ref[...]).astype(BF16)
    y = jax.nn.sigmoid(_dot(h, wga_ref[...])) * _dot(oa_ref[0], wpa_ref[...])
    y = y + jax.nn.sigmoid(_dot(h, wgb_ref[...])) * _dot(o_pool, wpb_ref[...])
    o_ref[0] = x + gt_ref[0] * _dot(y.astype(BF16), wo_ref[...])


def _merge(x, g, shift, scale, gate, o_att, pooled, w_pool, pool_scale, w_pa, w_pb, w_ga, w_gb, w_o):
    bsz, s, d = x.shape
    tm = min(s, 256)
    nt = s // tm
    row = lambda n: pl.BlockSpec((1, tm, n), lambda b, i: (b, i, 0))
    return pl.pallas_call(
        _merge_kernel,
        grid=(bsz, nt),
        in_specs=[row(d), pl.BlockSpec((1, d), lambda b, i: (0, 0)),
                  _mod_spec(shift, tm, nt), _mod_spec(scale, tm, nt), _mod_spec(gate, tm, nt),
                  row(o_att.shape[2]), row(pooled.shape[2]),
                  _vmem(), _vmem(), _vmem(), _vmem(), _vmem(), _vmem(), _vmem()],
        out_specs=row(d),
        out_shape=jax.ShapeDtypeStruct(x.shape, F32),
        compiler_params=_params("parallel", "parallel"),
        name="merge",
    )(x, g, shift, scale, gate, o_att, pooled, w_pool, pool_scale, w_pa, w_pb, w_ga, w_gb, w_o)


def _route_kernel(x_ref, g_ref, sh_ref, sc_ref, wr_ref, rb_ref, h_ref, idx_ref, w_ref):
    h = _rms_mod(x_ref[0], g_ref[...], sh_ref[0], sc_ref[0])
    h_ref[0] = h.astype(BF16)
    n_exp = wr_ref.shape[0]
    tm = h.shape[0]
    per = n_exp // N_GROUPS
    scores = jax.nn.sigmoid(_dot_nt(wr_ref[...], h, precision=lax.Precision.HIGHEST))
    sel = scores + rb_ref[...]
    neg = -jnp.inf
    groups = [sel[g * per:(g + 1) * per] for g in range(N_GROUPS)]
    eidx = lax.broadcasted_iota(jnp.int32, (per, tm), 0)
    gscore = []
    for grp in groups:
        m1 = jnp.max(grp, axis=0, keepdims=True)
        first = jnp.min(jnp.where(grp == m1, eidx, per), axis=0, keepdims=True)
        gscore.append(m1 + jnp.max(jnp.where(eidx == first, neg, grp), axis=0, keepdims=True))
    cand = []
    for g in range(N_GROUPS):
        rank = jnp.zeros((1, tm), jnp.int32)
        for o in range(N_GROUPS):
            if o != g:
                ahead = gscore[o] >= gscore[g] if o < g else gscore[o] > gscore[g]
                rank = rank + ahead.astype(jnp.int32)
        cand.append(jnp.where(rank < TOPK_GROUPS, groups[g], neg))
    cand = jnp.concatenate(cand, axis=0)
    row = lax.broadcasted_iota(jnp.int32, (n_exp, tm), 0)
    idxs, wts = [], []
    for _ in range(TOP_K):
        m = jnp.max(cand, axis=0, keepdims=True)
        pick = jnp.min(jnp.where(cand == m, row, n_exp), axis=0, keepdims=True)
        hit = row == pick
        idxs.append(pick)
        wts.append(jnp.sum(jnp.where(hit, scores, 0.0), axis=0, keepdims=True))
        cand = jnp.where(hit, neg, cand)
    wts = jnp.concatenate(wts, axis=0)
    idx_ref[...] = jnp.concatenate(idxs, axis=0)
    w_ref[...] = wts / jnp.sum(wts, axis=0, keepdims=True) * ROUTED_SCALE


def _route(x, g, shift, scale, w_router_t, router_bias):
    bsz, s, d = x.shape
    tm = min(s, 256)
    nt = s // tm
    col = pl.BlockSpec((TOP_K, tm), lambda b, i: (0, b * nt + i))
    return pl.pallas_call(
        _route_kernel,
        grid=(bsz, nt),
        in_specs=[pl.BlockSpec((1, tm, d), lambda b, i: (b, i, 0)), pl.BlockSpec((1, d), lambda b, i: (0, 0)),
                  _mod_spec(shift, tm, nt), _mod_spec(scale, tm, nt), _vmem(), _vmem()],
        out_specs=[pl.BlockSpec((1, tm, d), lambda b, i: (b, i, 0)), col, col],
        out_shape=[jax.ShapeDtypeStruct(x.shape, BF16),
                   jax.ShapeDtypeStruct((TOP_K, bsz * s), jnp.int32),
                   jax.ShapeDtypeStruct((TOP_K, bsz * s), F32)],
        compiler_params=_params("parallel", "parallel"),
        name="route",
    )(x, g, shift, scale, w_router_t, router_bias)


def _moe_kernel(blk_ref, exp_ref, lo_ref, hi_ref, lead_ref, nxt_ref, par_ref, xs_ref, wt_ref,
                wg_hbm, wu_hbm, wd_hbm, o_ref, acc_ref, wg_f, wu_f, wd_f, wg_b, wu_b, wd_b, sems):
    i = pl.program_id(0)
    lo = lo_ref[i]
    hi = hi_ref[i]

    def fetch(expert, slot):
        return [pltpu.make_async_copy(src.at[expert], dst.at[slot], sems.at[n, slot])
                for n, (src, dst) in enumerate(((wg_hbm, wg_f), (wu_hbm, wu_f), (wd_hbm, wd_f)))]

    @pl.when(i == 0)
    def _():
        for c in fetch(exp_ref[0], 0):
            c.start()

    @pl.when(lead_ref[i] == 1)
    def _():
        slot = par_ref[i]
        nxt = nxt_ref[i]

        @pl.when(nxt >= 0)
        def _():
            for c in fetch(nxt, 1 - slot):
                c.start()

        for c in fetch(exp_ref[i], slot):
            c.wait()
        wg_b[...] = wg_f[slot].astype(BF16)
        wu_b[...] = wu_f[slot].astype(BF16)
        wd_b[...] = wd_f[slot].astype(BF16)

    for j in range(MOE_SUBBLOCKS):
        r0 = j * MOE_ROWS

        @pl.when((lo < r0 + MOE_ROWS) & (hi > r0))
        def _(j=j, r0=r0):
            rows = pl.ds(r0, MOE_ROWS)
            xb = xs_ref[rows, :]
            row = r0 + lax.broadcasted_iota(jnp.int32, (MOE_ROWS, 1), 0)
            wt = jnp.where((row >= lo) & (row < hi), wt_ref[rows, :], 0.0)
            hid = (_silu(_dot(xb, wg_b[...])) * _dot(xb, wu_b[...])).astype(BF16)
            y = _dot(hid, wd_b[...]) * wt

            @pl.when(lo <= r0)
            def _():
                acc_ref[j] = y

            @pl.when(lo > r0)
            def _():
                acc_ref[j] = acc_ref[j] + y

            o_ref[rows, :] = acc_ref[j].astype(o_ref.dtype)


def _moe(xs, row_w, e_sorted, start, w_gate, w_up, w_down):
    n_assign, d = xs.shape
    n_exp, _, f = w_gate.shape
    n_blocks = -(-n_assign // MOE_BLOCK)
    cuts = jnp.sort(jnp.concatenate([jnp.arange(n_blocks, dtype=jnp.int32) * MOE_BLOCK, start[1:]]))
    n_items = cuts.shape[0]
    ends = jnp.concatenate([cuts[1:], jnp.full((1,), n_assign, jnp.int32)])
    blk = jnp.minimum(cuts // MOE_BLOCK, n_blocks - 1)
    expert = e_sorted[jnp.minimum(cuts, n_assign - 1)]
    lo, hi = cuts - blk * MOE_BLOCK, ends - blk * MOE_BLOCK
    live = ends > cuts
    item = jnp.arange(n_items, dtype=jnp.int32)
    change = jnp.concatenate([jnp.ones((1,), bool), expert[1:] != expert[:-1]])
    seen = jnp.cumsum(live.astype(jnp.int32))
    seen_at_run_start = lax.cummax(jnp.where(change, seen - live, 0))
    lead = live & (seen - seen_at_run_start == 1)
    parity = (jnp.cumsum(lead.astype(jnp.int32)) - 1) % 2
    lead_at = jnp.where(lead, item, n_items)
    nxt_item = lax.cummin(jnp.concatenate([lead_at[1:], jnp.full((1,), n_items, jnp.int32)]), reverse=True)
    nxt = jnp.where(nxt_item < n_items, expert[jnp.minimum(nxt_item, n_items - 1)], -1)
    row = lambda n: pl.BlockSpec((MOE_BLOCK, n), lambda i, blk, *_: (blk[i], 0))
    hbm = pl.BlockSpec(memory_space=pl.ANY)
    grid_spec = pltpu.PrefetchScalarGridSpec(
        num_scalar_prefetch=7, grid=(n_items,),
        in_specs=[row(d), row(1), hbm, hbm, hbm],
        out_specs=row(d),
        scratch_shapes=[pltpu.VMEM((MOE_SUBBLOCKS, MOE_ROWS, d), F32),
                        pltpu.VMEM((2, d, f), F32), pltpu.VMEM((2, d, f), F32), pltpu.VMEM((2, f, d), F32),
                        pltpu.VMEM((d, f), BF16), pltpu.VMEM((d, f), BF16), pltpu.VMEM((f, d), BF16),
                        pltpu.SemaphoreType.DMA((3, 2))])
    i32 = lambda z: z.astype(jnp.int32)
    return pl.pallas_call(
        _moe_kernel, grid_spec=grid_spec,
        out_shape=jax.ShapeDtypeStruct((n_assign, d), BF16),
        compiler_params=_params("arbitrary"),
        name="moe",
    )(i32(blk), i32(expert), i32(lo), i32(hi), i32(lead), i32(nxt), i32(parity),
      xs, row_w, w_gate, w_up, w_down)


def _final_kernel(x_ref, h_ref, r_ref, gt_ref, wg_ref, wu_ref, wd_ref, gf_ref, o_ref):
    hb = h_ref[0]
    hid = (_silu(_dot(hb, wg_ref[...])) * _dot(hb, wu_ref[...])).astype(BF16)
    moe = _dot(hid, wd_ref[...])
    for k in range(r_ref.shape[0]):
        moe = moe + r_ref[k].astype(F32)
    o_ref[0] = _rms(x_ref[0] + gt_ref[0] * moe, gf_ref[...])


def _final(x, h, routed, row0, gate, w_sg, w_su, w_sd, g_final):
    bsz, s, d = x.shape
    tm = min(s, 256)
    nt = s // tm
    assert row0 % tm == 0
    rb0 = row0 // tm
    row = pl.BlockSpec((1, tm, d), lambda b, i: (b, i, 0))
    return pl.pallas_call(
        _final_kernel,
        grid=(bsz, nt),
        in_specs=[row, row, pl.BlockSpec((routed.shape[0], tm, d), lambda b, i: (0, rb0 + b * nt + i, 0)),
                  _mod_spec(gate, tm, nt), _vmem(), _vmem(), _vmem(), pl.BlockSpec((1, d), lambda b, i: (0, 0))],
        out_specs=row,
        out_shape=jax.ShapeDtypeStruct(x.shape, F32),
        compiler_params=_params("parallel", "parallel"),
        name="final",
    )(x, h, routed, gate, w_sg, w_su, w_sd, g_final)


def kernel(x_prompt, x_sample, cache_k, cache_v, cache_logf, state_pool, page_table, c_prompt, c_sample, w_ada, b_ada, g_mix, w_in, b_f, w_pool, pool_scale, w_pa, w_pb, w_o, g_ffn, w_router, router_bias, w_gate, w_up, w_down, w_sh_gate, w_sh_up, w_sh_down, g_final):
    depth = w_ada.shape[0]
    assert depth == 1
    bsz, seq, d = x_prompt.shape
    db, t_new, _ = x_sample.shape
    _, n_pool_pages, page, n_heads, dh = cache_k.shape
    a = n_heads * dh
    p = state_pool.shape[3]
    n_exp = w_router.shape[2]
    n_dec = db * t_new

    w = w_in[0]
    cuts = [a, 2 * a, 3 * a, 3 * a + n_heads, 3 * a + n_heads + p, 3 * a + n_heads + p + d]
    wq, wk, wv, wf, wu, wga, wgb = [z.astype(BF16) for z in jnp.split(w, cuts, axis=1)]
    wq_t, wv_t = wq.T, wv.T
    wf = jnp.pad(wf, ((0, 0), (0, LANES - n_heads)))
    bf = jnp.pad(b_f[0], (0, LANES - n_heads)).reshape(1, LANES)
    wpool, wpa, wpb, wo = w_pool[0].astype(BF16), w_pa[0].astype(BF16), w_pb[0].astype(BF16), w_o[0].astype(BF16)
    wsg, wsu, wsd = w_sh_gate[0].astype(BF16), w_sh_up[0].astype(BF16), w_sh_down[0].astype(BF16)
    wr_t = w_router[0].T
    rbias = router_bias[0].reshape(n_exp, 1)
    gmix, gffn, gfin = g_mix[0].reshape(1, d), g_ffn[0].reshape(1, d), g_final.reshape(1, d)
    pscale = pool_scale[0].reshape(1, p)

    c_all = jnp.concatenate([c_prompt, c_sample], axis=0)
    r_pad = -c_all.shape[0] % 16
    mod = _ada(jnp.pad(c_all, ((0, r_pad), (0, 0))), w_ada[0], b_ada[0])
    mod_p = [m.reshape(bsz, 1, d) for m in jnp.split(mod[:bsz], 6, axis=-1)]
    mod_s = [jnp.repeat(m, t_new, axis=0).reshape(1, n_dec, d) for m in jnp.split(mod[bsz:bsz + db], 6, axis=-1)]

    def mixer_in(x, m, qscale):
        return _inproj(x, gmix, m[0], m[1], wq_t, wk, wv_t, wu, wf, bf, qscale)

    def mixer_out(x, m, o_att, pooled):
        x1 = _merge(x, gmix, m[0], m[1], m[2], o_att, pooled, wpool, pscale, wpa, wpb, wga, wgb, wo)
        return (x1,) + tuple(_route(x1, gffn, m[3], m[4], wr_t, rbias))

    qt_p, k_p, vt_p, u_p, lf_p = mixer_in(x_prompt, mod_p, dh ** -0.5 * LOG2E)
    logf_p = lf_p[:, :, :n_heads]
    cum_p = _cumsum_rows(logf_p.transpose(0, 2, 1))
    o_att_p = _attn_prompt(qt_p, k_p, vt_p, cum_p, dh)
    x1_p, h2_p, idx_p, wt_p = mixer_out(x_prompt, mod_p, o_att_p, _pool_prompt(u_p))

    xs3 = x_sample.reshape(1, n_dec, d)
    qt_s, k_s, vt_s, u_s, lf_s = mixer_in(xs3, mod_s, dh ** -0.5)
    q_s, v_s = qt_s[0].T, vt_s[0].T
    logf_s = lf_s[0, :, :n_heads].reshape(db, t_new, n_heads)
    u_s = u_s.reshape(db, t_new, p)
    cache_kt = cache_k[0].transpose(0, 2, 3, 1).reshape(n_pool_pages, a, page)
    cache_vt = cache_v[0].transpose(0, 2, 3, 1).reshape(n_pool_pages, a, page)
    o_att_s = _decode_attn(q_s.reshape(db, t_new, a), k_s.reshape(db, t_new, a), v_s.reshape(db, t_new, a),
                           logf_s, cache_kt, cache_vt, cache_logf[0].transpose(0, 2, 1), page_table, n_heads, dh)
    ext_s = jnp.concatenate([state_pool[0], u_s], axis=1)
    pooled_s = _pool_sample(ext_s.transpose(1, 0, 2), t_new).transpose(1, 0, 2).reshape(1, n_dec, p)
    x1_s, h2_s, idx_s, wt_s = mixer_out(xs3, mod_s, o_att_s.astype(BF16).reshape(1, n_dec, a), pooled_s)

    n_prompt = bsz * seq
    n_tok = n_prompt + n_dec
    h2_all = jnp.concatenate([h2_p.reshape(n_prompt, d), h2_s.reshape(n_dec, d)], axis=0)
    flat_e = jnp.concatenate([idx_p, idx_s], axis=1).T.reshape(-1)
    flat_w = jnp.concatenate([wt_p, wt_s], axis=1).T.reshape(-1)
    n_assign = n_tok * TOP_K
    ids = jnp.arange(n_assign, dtype=jnp.int32)
    e_sorted, order, w_sorted = lax.sort((flat_e, ids, flat_w), num_keys=1, is_stable=True)
    _, rank = lax.sort((order, ids), num_keys=1)
    start = jnp.searchsorted(e_sorted, jnp.arange(n_exp, dtype=jnp.int32), side='left').astype(jnp.int32)
    xs = h2_all[order // TOP_K]
    ys = _moe(xs, w_sorted.reshape(n_assign, 1), e_sorted, start, w_gate[0], w_up[0], w_down[0])
    routed = ys[rank.reshape(n_tok, TOP_K).T.reshape(-1)].reshape(TOP_K, n_tok, d)

    y_p = _final(x1_p, h2_p, routed, 0, mod_p[5], wsg, wsu, wsd, gfin)
    y_s = _final(x1_s, h2_s, routed, n_prompt, mod_s[5], wsg, wsu, wsd, gfin)

    heads = lambda z, b, s: z.reshape(1, b, s, n_heads, dh)
    pool_p = u_p[:, seq - (POOL_HALO - 1):][None]
    pool_s = ext_s[:, ext_s.shape[1] - (POOL_HALO - 1):][None]
    v_p = vt_p.reshape(bsz, n_heads, dh, seq).transpose(0, 3, 1, 2)[None]
    return (y_p, y_s.reshape(db, t_new, d), heads(k_p, bsz, seq), v_p, logf_p[None], pool_p,
            heads(k_s, db, t_new), heads(v_s, db, t_new), logf_s[None], pool_s)
```

```python
import functools

import jax
import jax.numpy as jnp
from jax import lax
from jax.experimental import pallas as pl
from jax.experimental.pallas import tpu as pltpu

F32 = jnp.float32
BF16 = jnp.bfloat16

RMS_EPS = 1e-6
LOG2E = 1.4426950408889634
TOP_K = 8
N_GROUPS = 8
TOPK_GROUPS = 4
ROUTED_SCALE = 2.5
POOL_WINDOWS = (2, 4, 8, 16)
POOL_HALO = 16
LANES = 128
DECODE_SLOTS = 3
MOE_ROWS = 128
MOE_SUBBLOCKS = 6
MOE_BLOCK = MOE_ROWS * MOE_SUBBLOCKS
VMEM_LIMIT = 56 * 1024 * 1024

_dot = functools.partial(jnp.dot, preferred_element_type=F32)


def _dot_nt(a, b, precision=None):
    return lax.dot_general(a, b, (((1,), (1,)), ((), ())), preferred_element_type=F32, precision=precision)


def _params(*sem):
    return pltpu.CompilerParams(dimension_semantics=sem, vmem_limit_bytes=VMEM_LIMIT)


def _vmem():
    return pl.BlockSpec(memory_space=pltpu.VMEM)


def _rms(x, g):
    return x * lax.rsqrt(jnp.mean(x * x, axis=-1, keepdims=True) + RMS_EPS) * g


def _rms_mod(x, g, shift, scale):
    return _rms(x, g) * (1.0 + scale) + shift


def _log_sigmoid(x):
    return jnp.minimum(x, 0.0) - jnp.log1p(jnp.exp(-jnp.abs(x)))


def _silu(x):
    return x * jax.nn.sigmoid(x)


def _cumsum_lanes(x):
    n = x.shape[-1]
    lane = lax.broadcasted_iota(jnp.int32, x.shape, x.ndim - 1)
    k = 1
    while k < n:
        x = x + jnp.where(lane >= k, pltpu.roll(x, k, axis=x.ndim - 1), 0.0)
        k *= 2
    return x


def _mod_spec(mod, tm, nt):
    _, r, d = mod.shape
    if r == 1:
        return pl.BlockSpec((1, 1, d), lambda b, i: (b, 0, 0))
    return pl.BlockSpec((1, tm, d), lambda b, i: (b, i, 0))


def _ada_kernel(c_ref, w_ref, b_ref, o_ref):
    a = _silu(c_ref[...]).astype(BF16)
    o_ref[...] = _dot(a, w_ref[...].astype(BF16)) + b_ref[...]


def _ada(c, w, b):
    r, d = c.shape
    n = w.shape[1]
    tn = min(n, 1024)
    return pl.pallas_call(
        _ada_kernel,
        grid=(n // tn,),
        in_specs=[pl.BlockSpec((r, d), lambda j: (0, 0)),
                  pl.BlockSpec((d, tn), lambda j: (0, j)),
                  pl.BlockSpec((1, tn), lambda j: (0, j))],
        out_specs=pl.BlockSpec((r, tn), lambda j: (0, j)),
        out_shape=jax.ShapeDtypeStruct((r, n), F32),
        compiler_params=_params("parallel"),
        name="ada",
    )(c, w, b.reshape(1, n))


def _inproj_kernel(x_ref, g_ref, sh_ref, sc_ref, wqt_ref, wk_ref, wvt_ref, wu_ref, wf_ref, bf_ref,
                   qt_ref, k_ref, vt_ref, u_ref, lf_ref, *, qscale):
    h = _rms_mod(x_ref[0], g_ref[...], sh_ref[0], sc_ref[0]).astype(BF16)
    qt_ref[0] = (_dot_nt(wqt_ref[...], h) * qscale).astype(BF16)
    vt_ref[0] = _dot_nt(wvt_ref[...], h)
    k_ref[0] = _dot(h, wk_ref[...])
    u_ref[0] = _dot(h, wu_ref[...])
    lf_ref[0] = _log_sigmoid(_dot(h, wf_ref[...]) + bf_ref[...])


def _inproj(x, g, shift, scale, wq_t, wk, wv_t, wu, wf, bf, qscale):
    bsz, s, d = x.shape
    a, p = wk.shape[1], wu.shape[1]
    tm = min(s, 512)
    nt = s // tm
    row = lambda n: pl.BlockSpec((1, tm, n), lambda b, i: (b, i, 0))
    col = pl.BlockSpec((1, a, tm), lambda b, i: (b, 0, i))
    return pl.pallas_call(
        functools.partial(_inproj_kernel, qscale=qscale),
        grid=(bsz, nt),
        in_specs=[row(d), pl.BlockSpec((1, d), lambda b, i: (0, 0)),
                  _mod_spec(shift, tm, nt), _mod_spec(scale, tm, nt),
                  _vmem(), _vmem(), _vmem(), _vmem(), _vmem(), _vmem()],
        out_specs=[col, row(a), col, row(p), row(LANES)],
        out_shape=[jax.ShapeDtypeStruct((bsz, a, s), BF16),
                   jax.ShapeDtypeStruct((bsz, s, a), F32),
                   jax.ShapeDtypeStruct((bsz, a, s), F32),
                   jax.ShapeDtypeStruct((bsz, s, p), F32),
                   jax.ShapeDtypeStruct((bsz, s, LANES), F32)],
        compiler_params=_params("parallel", "parallel"),
        name="inproj",
    )(x, g, shift, scale, wq_t, wk, wv_t, wu, wf, bf)


def _cumsum_kernel(x_ref, o_ref):
    o_ref[0] = _cumsum_lanes(x_ref[0])


def _cumsum_rows(x):
    bsz, h, s = x.shape
    spec = pl.BlockSpec((1, h, s), lambda b: (b, 0, 0))
    return pl.pallas_call(
        _cumsum_kernel, grid=(bsz,), in_specs=[spec], out_specs=spec,
        out_shape=jax.ShapeDtypeStruct(x.shape, F32),
        compiler_params=_params("parallel"), name="logf_cumsum",
    )(x)


def _attn_prompt_kernel(qt_ref, k_ref, vt_ref, c_ref, o_ref, m_ref, l_ref, acc_ref, *, t, dh):
    qi = pl.program_id(2)
    qt = qt_ref[0]
    row = lax.broadcasted_iota(jnp.int32, (2 * dh, 1), 0)
    zero = jnp.zeros_like(qt)
    qh = (jnp.where(row < dh, qt, zero), jnp.where(row < dh, zero, qt))
    m_ref[...] = jnp.full(m_ref.shape, -jnp.inf, F32)
    l_ref[...] = jnp.zeros(l_ref.shape, F32)
    acc_ref[...] = jnp.zeros(acc_ref.shape, F32)

    def tile(ki, diagonal):
        k0 = pl.multiple_of(ki * t, t)
        kb = k_ref[0, pl.ds(k0, t), :].astype(BF16)
        vt = vt_ref[0, :, pl.ds(k0, t)].astype(BF16)
        cb = c_ref[0, 0, ki] * LOG2E
        pv, corr = [], []
        for j in range(2):
            s = _dot(kb, qh[j]) - cb[:, j:j + 1]
            if diagonal:
                key = lax.broadcasted_iota(jnp.int32, (t, t), 0)
                qry = lax.broadcasted_iota(jnp.int32, (t, t), 1)
                s = jnp.where(key <= qry, s, -jnp.inf)
            m_old = m_ref[j]
            m_new = jnp.maximum(m_old, jnp.max(s, axis=0, keepdims=True))
            p = jnp.exp2(s - m_new)
            cj = jnp.exp2(m_old - m_new)
            l_ref[j] = l_ref[j] * cj + jnp.sum(p, axis=0, keepdims=True)
            m_ref[j] = m_new
            pv.append(_dot(vt[j * dh:(j + 1) * dh], p.astype(BF16)))
            corr.append(jnp.broadcast_to(cj, (dh, t)))
        acc_ref[...] = acc_ref[...] * jnp.concatenate(corr, axis=0) + jnp.concatenate(pv, axis=0)

    def body(ki, carry):
        tile(ki, False)
        return carry

    lax.fori_loop(0, qi, body, 0)
    tile(qi, True)
    denom = jnp.concatenate([jnp.broadcast_to(l_ref[j], (dh, t)) for j in range(2)], axis=0)
    o_ref[0] = (acc_ref[...] / denom).T.astype(o_ref.dtype)


def _attn_prompt(qt, k, vt, cum, dh):
    bsz, s, a = k.shape
    hp = a // (2 * dh)
    t = min(s, 512)
    nt = s // t
    cum = cum.reshape(bsz, hp, 2, nt, t).transpose(0, 1, 3, 4, 2)
    return pl.pallas_call(
        functools.partial(_attn_prompt_kernel, t=t, dh=dh),
        grid=(bsz, hp, nt),
        in_specs=[pl.BlockSpec((1, 2 * dh, t), lambda b, h, i: (b, h, i)),
                  pl.BlockSpec((1, s, 2 * dh), lambda b, h, i: (b, 0, h)),
                  pl.BlockSpec((1, 2 * dh, s), lambda b, h, i: (b, h, 0)),
                  pl.BlockSpec((1, 1, nt, t, 2), lambda b, h, i: (b, h, 0, 0, 0))],
        out_specs=pl.BlockSpec((1, t, 2 * dh), lambda b, h, i: (b, i, h)),
        out_shape=jax.ShapeDtypeStruct((bsz, s, a), BF16),
        scratch_shapes=[pltpu.VMEM((2, 1, t), F32), pltpu.VMEM((2, 1, t), F32),
                        pltpu.VMEM((2 * dh, t), F32)],
        compiler_params=_params("parallel", "parallel", "parallel"),
        name="attn_prompt",
    )(qt, k, vt, cum)


def _logf_pages_kernel(x_ref, rin_ref, tot_ref):
    x = x_ref[...]
    n = x.shape[1]
    key = lax.broadcasted_iota(jnp.int32, (n, 2 * n), 0)
    out = lax.broadcasted_iota(jnp.int32, (n, 2 * n), 1)
    sel = jnp.where((key > out) | (out >= n), 1.0, 0.0).astype(BF16)
    acc = jnp.zeros((x.shape[0], 2 * n), F32)
    rest = x
    for _ in range(3):
        part = rest.astype(BF16)
        acc = acc + _dot(part, sel)
        rest = rest - part.astype(F32)
    rin_ref[...] = acc[:, :n]
    tot_ref[...] = acc[:, n:]


def _logf_pages(lf_t):
    n_pool, n_heads, page = lf_t.shape
    pb = 64 if n_pool % 64 == 0 else n_pool
    spec = pl.BlockSpec((pb * n_heads, page), lambda i: (i, 0))
    rin, tot = pl.pallas_call(
        _logf_pages_kernel,
        grid=(n_pool // pb,), in_specs=[spec], out_specs=[spec, spec],
        out_shape=[jax.ShapeDtypeStruct((n_pool * n_heads, page), F32)] * 2,
        compiler_params=_params("parallel"), name="logf_pages",
    )(lf_t.reshape(n_pool * n_heads, page))
    return rin.reshape(lf_t.shape), tot.reshape(lf_t.shape)


def _decode_kernel(pt_ref, qb_ref, kn_ref, vn_ref, lfn_ref, *refs, g_pages, n_pages, n_new, n_heads, dh):
    rin_refs = refs[:g_pages]
    tot_refs = refs[g_pages:2 * g_pages]
    k_hbm, v_hbm, o_ref, m_ref, l_ref, acc_ref, carry_ref, k_buf, v_buf, sems = refs[2 * g_pages:]
    step = pl.program_id(1)
    steps = pl.num_programs(1)
    t = pl.program_id(0) * steps + step
    total = pl.num_programs(0) * steps
    qb = qb_ref[0]

    def fetch(t_of, slot):
        seq, stp = t_of // steps, t_of % steps
        copies = []
        for g in range(g_pages):
            pg = pt_ref[seq * n_pages + n_pages - 1 - (stp * g_pages + g)]
            copies.append(pltpu.make_async_copy(k_hbm.at[pg], k_buf.at[slot, g], sems.at[0, slot]))
            copies.append(pltpu.make_async_copy(v_hbm.at[pg], v_buf.at[slot, g], sems.at[1, slot]))
        return copies

    @pl.when(t == 0)
    def _():
        for ahead in range(DECODE_SLOTS - 1):
            @pl.when(ahead < total)
            def _(ahead=ahead):
                for c in fetch(ahead, ahead):
                    c.start()

    @pl.when(t + DECODE_SLOTS - 1 < total)
    def _():
        for c in fetch(t + DECODE_SLOTS - 1, (t + DECODE_SLOTS - 1) % DECODE_SLOTS):
            c.start()

    @pl.when(step == 0)
    def _():
        c = _cumsum_lanes(lfn_ref[0])
        tot = c[:, LANES - 1:LANES]
        rnew = (tot - c)[:, :16]
        s = _dot_nt(qb, kn_ref[0]) + jnp.concatenate([rnew] * n_new, axis=0)
        tok = lax.broadcasted_iota(jnp.int32, s.shape, 0) // n_heads
        key = lax.broadcasted_iota(jnp.int32, s.shape, 1)
        s = jnp.where(key <= tok, s, -jnp.inf)
        m = jnp.max(s, axis=-1, keepdims=True)
        p = jnp.exp(s - m)
        m_ref[...] = m
        l_ref[...] = jnp.sum(p, axis=-1, keepdims=True)
        acc_ref[...] = _dot(p.astype(BF16), vn_ref[0])
        carry_ref[...] = jnp.broadcast_to(tot, carry_ref.shape)

    slot = t % DECODE_SLOTS
    for c in fetch(t, slot):
        c.wait()
    carry = carry_ref[...]
    scores = []
    for g in range(g_pages):
        r = carry + rin_refs[g][0]
        carry = carry + tot_refs[g][0]
        kb = k_buf[slot, g].astype(BF16)
        scores.append(_dot(qb, kb) + jnp.concatenate([r] * n_new, axis=0))
    carry_ref[...] = carry
    s = jnp.concatenate(scores, axis=1)
    m_old = m_ref[...]
    m_new = jnp.maximum(m_old, jnp.max(s, axis=-1, keepdims=True))
    p = jnp.exp(s - m_new)
    corr = jnp.exp(m_old - m_new)
    l_ref[...] = l_ref[...] * corr + jnp.sum(p, axis=-1, keepdims=True)
    m_ref[...] = m_new
    pb = p.astype(BF16)
    page = s.shape[1] // g_pages
    pv = _dot_nt(pb[:, :page], v_buf[slot, 0].astype(BF16))
    for g in range(1, g_pages):
        pv = pv + _dot_nt(pb[:, g * page:(g + 1) * page], v_buf[slot, g].astype(BF16))
    acc_ref[...] = acc_ref[...] * corr + pv

    @pl.when(step == steps - 1)
    def _():
        o = acc_ref[...] / l_ref[...]
        head = lax.broadcasted_iota(jnp.int32, o.shape, 0) % n_heads
        lane_head = lax.broadcasted_iota(jnp.int32, o.shape, 1) // dh
        o = jnp.where(head == lane_head, o, 0.0)
        o_ref[0] = jnp.sum(o.reshape(n_new, n_heads, o.shape[1]), axis=1)


def _decode_attn(q, k, v, logf, cache_kt, cache_vt, cache_logf_t, page_table, n_heads, dh):
    db, t, a = q.shape
    n_pages = page_table.shape[1]
    page = cache_kt.shape[2]
    assert page == LANES and t <= 16
    g_pages = max(g for g in (8, 4, 2, 1) if n_pages % g == 0)
    rows = t * n_heads
    rin, tot = _logf_pages(cache_logf_t)
    eye = jnp.eye(n_heads, dtype=BF16)
    qb = (q.reshape(db, t, 1, n_heads, dh) * eye[None, None, :, :, None]).reshape(db, rows, a)
    pad = lambda z: jnp.pad(z.astype(BF16), ((0, 0), (0, 16 - t), (0, 0)))
    lfn = jnp.pad(logf.transpose(0, 2, 1), ((0, 0), (0, 0), (0, LANES - t)))

    def page_idx(g):
        return lambda b, s, pt: (pt[b * n_pages + n_pages - 1 - (s * g_pages + g)], 0, 0)

    seq = lambda shape: pl.BlockSpec((1,) + shape, lambda b, s, pt: (b, 0, 0))
    hbm = pl.BlockSpec(memory_space=pl.ANY)
    in_specs = [seq((rows, a)), seq((16, a)), seq((16, a)), seq((n_heads, LANES))]
    for _ in range(2):
        in_specs += [pl.BlockSpec((1, n_heads, page), page_idx(g)) for g in range(g_pages)]
    in_specs += [hbm, hbm]
    grid_spec = pltpu.PrefetchScalarGridSpec(
        num_scalar_prefetch=1, grid=(db, n_pages // g_pages), in_specs=in_specs,
        out_specs=seq((t, a)),
        scratch_shapes=[pltpu.VMEM((rows, 1), F32), pltpu.VMEM((rows, 1), F32),
                        pltpu.VMEM((rows, a), F32), pltpu.VMEM((n_heads, page), F32),
                        pltpu.VMEM((DECODE_SLOTS, g_pages, a, page), F32),
                        pltpu.VMEM((DECODE_SLOTS, g_pages, a, page), F32),
                        pltpu.SemaphoreType.DMA((2, DECODE_SLOTS))])
    return pl.pallas_call(
        functools.partial(_decode_kernel, g_pages=g_pages, n_pages=n_pages, n_new=t, n_heads=n_heads, dh=dh),
        grid_spec=grid_spec,
        out_shape=jax.ShapeDtypeStruct((db, t, a), F32),
        compiler_params=_params("arbitrary", "arbitrary"),
        name="attn_decode",
    )(page_table.reshape(-1), qb, pad(k), pad(v), lfn, *([rin] * g_pages), *([tot] * g_pages), cache_kt, cache_vt)


def _pool_prompt_kernel(u_ref, halo_ref, o_ref, *, tm, pg):
    i = pl.program_id(1)
    u = u_ref[0]
    halo = jnp.where(i > 0, halo_ref[0], 0.0)
    ext = jnp.concatenate([halo, u], axis=0)
    pos = i * tm + lax.broadcasted_iota(jnp.int32, (tm, 1), 0)
    outs = []
    for g, w in enumerate(POOL_WINDOWS):
        r = ext[:, g * pg:(g + 1) * pg]
        k = 1
        while k < w:
            r = r + pltpu.roll(r, k, axis=0)
            k *= 2
        count = jnp.minimum(w, pos + 1).astype(F32)
        outs.append(r[POOL_HALO:] / count - u[:, g * pg:(g + 1) * pg])
    o_ref[0] = jnp.concatenate(outs, axis=-1).astype(o_ref.dtype)


def _pool_prompt(u):
    bsz, s, p = u.shape
    tm = min(s, 512)
    hb = tm // POOL_HALO
    return pl.pallas_call(
        functools.partial(_pool_prompt_kernel, tm=tm, pg=p // len(POOL_WINDOWS)),
        grid=(bsz, s // tm),
        in_specs=[pl.BlockSpec((1, tm, p), lambda b, i: (b, i, 0)),
                  pl.BlockSpec((1, POOL_HALO, p), lambda b, i: (b, jnp.maximum(i * hb - 1, 0), 0))],
        out_specs=pl.BlockSpec((1, tm, p), lambda b, i: (b, i, 0)),
        out_shape=jax.ShapeDtypeStruct(u.shape, BF16),
        compiler_params=_params("parallel", "parallel"),
        name="pool_prompt",
    )(u, u)


def _pool_sample_kernel(e_ref, o_ref, *, n_new, n_prev, pg):
    for t in range(n_new):
        e = n_prev + t
        outs = []
        for g, w in enumerate(POOL_WINDOWS):
            cols = slice(g * pg, (g + 1) * pg)
            lo = max(0, e - w + 1)
            acc = e_ref[lo, :, cols]
            for j in range(lo + 1, e + 1):
                acc = acc + e_ref[j, :, cols]
            outs.append(acc / float(min(w, e + 1)) - e_ref[e, :, cols])
        o_ref[t] = jnp.concatenate(outs, axis=-1).astype(o_ref.dtype)


def _pool_sample(ext_t, n_new):
    n_all, db, p = ext_t.shape
    return pl.pallas_call(
        functools.partial(_pool_sample_kernel, n_new=n_new, n_prev=n_all - n_new, pg=p // len(POOL_WINDOWS)),
        in_specs=[_vmem()], out_specs=_vmem(),
        out_shape=jax.ShapeDtypeStruct((n_new, db, p), BF16),
        compiler_params=pltpu.CompilerParams(vmem_limit_bytes=VMEM_LIMIT),
        name="pool_sample",
    )(ext_t)


def _merge_kernel(x_ref, g_ref, sh_ref, sc_ref, gt_ref, oa_ref, pl_ref, wpool_ref, ps_ref,
                  wpa_ref, wpb_ref, wga_ref, wgb_ref, wo_ref, o_ref):
    x = x_ref[0]
    h = _rms_mod(x, g_ref[...], sh_ref[0], sc_ref[0]).astype(BF16)
    pooled = pl_ref[0]
    n_groups, pg, _ = wpool_ref.shape
    mixed = jnp.concatenate(
        [_dot(pooled[:, g * pg:(g + 1) * pg], wpool_ref[g]) for g in range(n_groups)], axis=-1)
    o_pool = (mixed * ps_ref[...]).astype(BF16)
    y = jax.nn.sigmoid(_dot(h, wga_ref[...])) * _dot(oa_ref[0], wpa_ref[...])
    y = y + jax.nn.sigmoid(_dot(h, wgb_ref[...])) * _dot(o_pool, wpb_ref[...])
    o_ref[0] = x + gt_ref[0] * _dot(y.astype(BF16), wo_ref[...])


def _merge(x, g, shift, scale, gate, o_att, pooled, w_pool, pool_scale, w_pa, w_pb, w_ga, w_gb, w_o):
    bsz, s, d = x.shape
    tm = min(s, 256)
    nt = s // tm
    row = lambda n: pl.BlockSpec((1, tm, n), lambda b, i: (b, i, 0))
    return pl.pallas_call(
        _merge_kernel,
        grid=(bsz, nt),
        in_specs=[row(d), pl.BlockSpec((1, d), lambda b, i: (0, 0)),
                  _mod_spec(shift, tm, nt), _mod_spec(scale, tm, nt), _mod_spec(gate, tm, nt),
                  row(o_att.shape[2]), row(pooled.shape[2]),
                  _vmem(), _vmem(), _vmem(), _vmem(), _vmem(), _vmem(), _vmem()],
        out_specs=row(d),
        out_shape=jax.ShapeDtypeStruct(x.shape, F32),
        compiler_params=_params("parallel", "parallel"),
        name="merge",
    )(x, g, shift, scale, gate, o_att, pooled, w_pool, pool_scale, w_pa, w_pb, w_ga, w_gb, w_o)


def _route_kernel(x_ref, g_ref, sh_ref, sc_ref, wr_ref, rb_ref, h_ref, idx_ref, w_ref):
    h = _rms_mod(x_ref[0], g_ref[...], sh_ref[0], sc_ref[0])
    h_ref[0] = h.astype(BF16)
    n_exp = wr_ref.shape[0]
    tm = h.shape[0]
    per = n_exp // N_GROUPS
    scores = jax.nn.sigmoid(_dot_nt(wr_ref[...], h, precision=lax.Precision.HIGHEST))
    sel = scores + rb_ref[...]
    neg = -jnp.inf
    groups = [sel[g * per:(g + 1) * per] for g in range(N_GROUPS)]
    eidx = lax.broadcasted_iota(jnp.int32, (per, tm), 0)
    gscore = []
    for grp in groups:
        m1 = jnp.max(grp, axis=0, keepdims=True)
        first = jnp.min(jnp.where(grp == m1, eidx, per), axis=0, keepdims=True)
        gscore.append(m1 + jnp.max(jnp.where(eidx == first, neg, grp), axis=0, keepdims=True))
    cand = []
    for g in range(N_GROUPS):
        rank = jnp.zeros((1, tm), jnp.int32)
        for o in range(N_GROUPS):
            if o != g:
                ahead = gscore[o] >= gscore[g] if o < g else gscore[o] > gscore[g]
                rank = rank + ahead.astype(jnp.int32)
        cand.append(jnp.where(rank < TOPK_GROUPS, groups[g], neg))
    cand = jnp.concatenate(cand, axis=0)
    row = lax.broadcasted_iota(jnp.int32, (n_exp, tm), 0)
    idxs, wts = [], []
    for _ in range(TOP_K):
        m = jnp.max(cand, axis=0, keepdims=True)
        pick = jnp.min(jnp.where(cand == m, row, n_exp), axis=0, keepdims=True)
        hit = row == pick
        idxs.append(pick)
        wts.append(jnp.sum(jnp.where(hit, scores, 0.0), axis=0, keepdims=True))
        cand = jnp.where(hit, neg, cand)
    wts = jnp.concatenate(wts, axis=0)
    idx_ref[...] = jnp.concatenate(idxs, axis=0)
    w_ref[...] = wts / jnp.sum(wts, axis=0, keepdims=True) * ROUTED_SCALE


def _route(x, g, shift, scale, w_router_t, router_bias):
    bsz, s, d = x.shape
    tm = min(s, 256)
    nt = s // tm
    col = pl.BlockSpec((TOP_K, tm), lambda b, i: (0, b * nt + i))
    return pl.pallas_call(
        _route_kernel,
        grid=(bsz, nt),
        in_specs=[pl.BlockSpec((1, tm, d), lambda b, i: (b, i, 0)), pl.BlockSpec((1, d), lambda b, i: (0, 0)),
                  _mod_spec(shift, tm, nt), _mod_spec(scale, tm, nt), _vmem(), _vmem()],
        out_specs=[pl.BlockSpec((1, tm, d), lambda b, i: (b, i, 0)), col, col],
        out_shape=[jax.ShapeDtypeStruct(x.shape, BF16),
                   jax.ShapeDtypeStruct((TOP_K, bsz * s), jnp.int32),
                   jax.ShapeDtypeStruct((TOP_K, bsz * s), F32)],
        compiler_params=_params("parallel", "parallel"),
        name="route",
    )(x, g, shift, scale, w_router_t, router_bias)


def _moe_kernel(blk_ref, exp_ref, lo_ref, hi_ref, lead_ref, nxt_ref, par_ref, xs_ref, wt_ref,
                wg_hbm, wu_hbm, wd_hbm, o_ref, acc_ref, wg_f, wu_f, wd_f, wg_b, wu_b, wd_b, sems):
    i = pl.program_id(0)
    lo = lo_ref[i]
    hi = hi_ref[i]

    def fetch(expert, slot):
        return [pltpu.make_async_copy(src.at[expert], dst.at[slot], sems.at[n, slot])
                for n, (src, dst) in enumerate(((wg_hbm, wg_f), (wu_hbm, wu_f), (wd_hbm, wd_f)))]

    @pl.when(i == 0)
    def _():
        for c in fetch(exp_ref[0], 0):
            c.start()

    @pl.when(lead_ref[i] == 1)
    def _():
        slot = par_ref[i]
        nxt = nxt_ref[i]

        @pl.when(nxt >= 0)
        def _():
            for c in fetch(nxt, 1 - slot):
                c.start()

        for c in fetch(exp_ref[i], slot):
            c.wait()
        wg_b[...] = wg_f[slot].astype(BF16)
        wu_b[...] = wu_f[slot].astype(BF16)
        wd_b[...] = wd_f[slot].astype(BF16)

    for j in range(MOE_SUBBLOCKS):
        r0 = j * MOE_ROWS

        @pl.when((lo < r0 + MOE_ROWS) & (hi > r0))
        def _(j=j, r0=r0):
            rows = pl.ds(r0, MOE_ROWS)
            xb = xs_ref[rows, :]
            row = r0 + lax.broadcasted_iota(jnp.int32, (MOE_ROWS, 1), 0)
            wt = jnp.where((row >= lo) & (row < hi), wt_ref[rows, :], 0.0)
            hid = (_silu(_dot(xb, wg_b[...])) * _dot(xb, wu_b[...])).astype(BF16)
            y = _dot(hid, wd_b[...]) * wt

            @pl.when(lo <= r0)
            def _():
                acc_ref[j] = y

            @pl.when(lo > r0)
            def _():
                acc_ref[j] = acc_ref[j] + y

            o_ref[rows, :] = acc_ref[j].astype(o_ref.dtype)


def _moe(xs, row_w, e_sorted, start, w_gate, w_up, w_down):
    n_assign, d = xs.shape
    n_exp, _, f = w_gate.shape
    n_blocks = -(-n_assign // MOE_BLOCK)
    cuts = jnp.sort(jnp.concatenate([jnp.arange(n_blocks, dtype=jnp.int32) * MOE_BLOCK, start[1:]]))
    n_items = cuts.shape[0]
    ends = jnp.concatenate([cuts[1:], jnp.full((1,), n_assign, jnp.int32)])
    blk = jnp.minimum(cuts // MOE_BLOCK, n_blocks - 1)
    expert = e_sorted[jnp.minimum(cuts, n_assign - 1)]
    lo, hi = cuts - blk * MOE_BLOCK, ends - blk * MOE_BLOCK
    live = ends > cuts
    item = jnp.arange(n_items, dtype=jnp.int32)
    change = jnp.concatenate([jnp.ones((1,), bool), expert[1:] != expert[:-1]])
    seen = jnp.cumsum(live.astype(jnp.int32))
    seen_at_run_start = lax.cummax(jnp.where(change, seen - live, 0))
    lead = live & (seen - seen_at_run_start == 1)
    parity = (jnp.cumsum(lead.astype(jnp.int32)) - 1) % 2
    lead_at = jnp.where(lead, item, n_items)
    nxt_item = lax.cummin(jnp.concatenate([lead_at[1:], jnp.full((1,), n_items, jnp.int32)]), reverse=True)
    nxt = jnp.where(nxt_item < n_items, expert[jnp.minimum(nxt_item, n_items - 1)], -1)
    row = lambda n: pl.BlockSpec((MOE_BLOCK, n), lambda i, blk, *_: (blk[i], 0))
    hbm = pl.BlockSpec(memory_space=pl.ANY)
    grid_spec = pltpu.PrefetchScalarGridSpec(
        num_scalar_prefetch=7, grid=(n_items,),
        in_specs=[row(d), row(1), hbm, hbm, hbm],
        out_specs=row(d),
        scratch_shapes=[pltpu.VMEM((MOE_SUBBLOCKS, MOE_ROWS, d), F32),
                        pltpu.VMEM((2, d, f), F32), pltpu.VMEM((2, d, f), F32), pltpu.VMEM((2, f, d), F32),
                        pltpu.VMEM((d, f), BF16), pltpu.VMEM((d, f), BF16), pltpu.VMEM((f, d), BF16),
                        pltpu.SemaphoreType.DMA((3, 2))])
    i32 = lambda z: z.astype(jnp.int32)
    return pl.pallas_call(
        _moe_kernel, grid_spec=grid_spec,
        out_shape=jax.ShapeDtypeStruct((n_assign, d), BF16),
        compiler_params=_params("arbitrary"),
        name="moe",
    )(i32(blk), i32(expert), i32(lo), i32(hi), i32(lead), i32(nxt), i32(parity),
      xs, row_w, w_gate, w_up, w_down)


def _final_kernel(x_ref, h_ref, r_ref, gt_ref, wg_ref, wu_ref, wd_ref, gf_ref, o_ref):
    hb = h_ref[0]
    hid = (_silu(_dot(hb, wg_ref[...])) * _dot(hb, wu_ref[...])).astype(BF16)
    moe = _dot(hid, wd_ref[...])
    for k in range(r_ref.shape[0]):
        moe = moe + r_ref[k].astype(F32)
    o_ref[0] = _rms(x_ref[0] + gt_ref[0] * moe, gf_ref[...])


def _final(x, h, routed, row0, gate, w_sg, w_su, w_sd, g_final):
    bsz, s, d = x.shape
    tm = min(s, 256)
    nt = s // tm
    assert row0 % tm == 0
    rb0 = row0 // tm
    row = pl.BlockSpec((1, tm, d), lambda b, i: (b, i, 0))
    return pl.pallas_call(
        _final_kernel,
        grid=(bsz, nt),
        in_specs=[row, row, pl.BlockSpec((routed.shape[0], tm, d), lambda b, i: (0, rb0 + b * nt + i, 0)),
                  _mod_spec(gate, tm, nt), _vmem(), _vmem(), _vmem(), pl.BlockSpec((1, d), lambda b, i: (0, 0))],
        out_specs=row,
        out_shape=jax.ShapeDtypeStruct(x.shape, F32),
        compiler_params=_params("parallel", "parallel"),
        name="final",
    )(x, h, routed, gate, w_sg, w_su, w_sd, g_final)


def kernel(x_prompt, x_sample, cache_k, cache_v, cache_logf, state_pool, page_table, c_prompt, c_sample, w_ada, b_ada, g_mix, w_in, b_f, w_pool, pool_scale, w_pa, w_pb, w_o, g_ffn, w_router, router_bias, w_gate, w_up, w_down, w_sh_gate, w_sh_up, w_sh_down, g_final):
    depth = w_ada.shape[0]
    assert depth == 1
    bsz, seq, d = x_prompt.shape
    db, t_new, _ = x_sample.shape
    _, n_pool_pages, page, n_heads, dh = cache_k.shape
    a = n_heads * dh
    p = state_pool.shape[3]
    n_exp = w_router.shape[2]
    n_dec = db * t_new

    w = w_in[0]
    cuts = [a, 2 * a, 3 * a, 3 * a + n_heads, 3 * a + n_heads + p, 3 * a + n_heads + p + d]
    wq, wk, wv, wf, wu, wga, wgb = [z.astype(BF16) for z in jnp.split(w, cuts, axis=1)]
    wq_t, wv_t = wq.T, wv.T
    wf = jnp.pad(wf, ((0, 0), (0, LANES - n_heads)))
    bf = jnp.pad(b_f[0], (0, LANES - n_heads)).reshape(1, LANES)
    wpool, wpa, wpb, wo = w_pool[0].astype(BF16), w_pa[0].astype(BF16), w_pb[0].astype(BF16), w_o[0].astype(BF16)
    wsg, wsu, wsd = w_sh_gate[0].astype(BF16), w_sh_up[0].astype(BF16), w_sh_down[0].astype(BF16)
    wr_t = w_router[0].T
    rbias = router_bias[0].reshape(n_exp, 1)
    gmix, gffn, gfin = g_mix[0].reshape(1, d), g_ffn[0].reshape(1, d), g_final.reshape(1, d)
    pscale = pool_scale[0].reshape(1, p)

    c_all = jnp.concatenate([c_prompt, c_sample], axis=0)
    r_pad = -c_all.shape[0] % 16
    mod = _ada(jnp.pad(c_all, ((0, r_pad), (0, 0))), w_ada[0], b_ada[0])
    mod_p = [m.reshape(bsz, 1, d) for m in jnp.split(mod[:bsz], 6, axis=-1)]
    mod_s = [jnp.repeat(m, t_new, axis=0).reshape(1, n_dec, d) for m in jnp.split(mod[bsz:bsz + db], 6, axis=-1)]

    def mixer_in(x, m, qscale):
        return _inproj(x, gmix, m[0], m[1], wq_t, wk, wv_t, wu, wf, bf, qscale)

    def mixer_out(x, m, o_att, pooled):
        x1 = _merge(x, gmix, m[0], m[1], m[2], o_att, pooled, wpool, pscale, wpa, wpb, wga, wgb, wo)
        return (x1,) + tuple(_route(x1, gffn, m[3], m[4], wr_t, rbias))

    qt_p, k_p, vt_p, u_p, lf_p = mixer_in(x_prompt, mod_p, dh ** -0.5 * LOG2E)
    logf_p = lf_p[:, :, :n_heads]
    cum_p = _cumsum_rows(logf_p.transpose(0, 2, 1))
    o_att_p = _attn_prompt(qt_p, k_p, vt_p, cum_p, dh)
    x1_p, h2_p, idx_p, wt_p = mixer_out(x_prompt, mod_p, o_att_p, _pool_prompt(u_p))

    xs3 = x_sample.reshape(1, n_dec, d)
    qt_s, k_s, vt_s, u_s, lf_s = mixer_in(xs3, mod_s, dh ** -0.5)
    q_s, v_s = qt_s[0].T, vt_s[0].T
    logf_s = lf_s[0, :, :n_heads].reshape(db, t_new, n_heads)
    u_s = u_s.reshape(db, t_new, p)
    cache_kt = cache_k[0].transpose(0, 2, 3, 1).reshape(n_pool_pages, a, page)
    cache_vt = cache_v[0].transpose(0, 2, 3, 1).reshape(n_pool_pages, a, page)
    o_att_s = _decode_attn(q_s.reshape(db, t_new, a), k_s.reshape(db, t_new, a), v_s.reshape(db, t_new, a),
                           logf_s, cache_kt, cache_vt, cache_logf[0].transpose(0, 2, 1), page_table, n_heads, dh)
    ext_s = jnp.concatenate([state_pool[0], u_s], axis=1)
    pooled_s = _pool_sample(ext_s.transpose(1, 0, 2), t_new).transpose(1, 0, 2).reshape(1, n_dec, p)
    x1_s, h2_s, idx_s, wt_s = mixer_out(xs3, mod_s, o_att_s.astype(BF16).reshape(1, n_dec, a), pooled_s)

    n_prompt = bsz * seq
    n_tok = n_prompt + n_dec
    h2_all = jnp.concatenate([h2_p.reshape(n_prompt, d), h2_s.reshape(n_dec, d)], axis=0)
    flat_e = jnp.concatenate([idx_p, idx_s], axis=1).T.reshape(-1)
    flat_w = jnp.concatenate([wt_p, wt_s], axis=1).T.reshape(-1)
    n_assign = n_tok * TOP_K
    ids = jnp.arange(n_assign, dtype=jnp.int32)
    e_sorted, order, w_sorted = lax.sort((flat_e, ids, flat_w), num_keys=1, is_stable=True)
    _, rank = lax.sort((order, ids), num_keys=1)
    start = jnp.searchsorted(e_sorted, jnp.arange(n_exp, dtype=jnp.int32), side='left').astype(jnp.int32)
    xs = h2_all[order // TOP_K]
    ys = _moe(xs, w_sorted.reshape(n_assign, 1), e_sorted, start, w_gate[0], w_up[0], w_down[0])
    routed = ys[rank.reshape(n_tok, TOP_K).T.reshape(-1)].reshape(TOP_K, n_tok, d)

    y_p = _final(x1_p, h2_p, routed, 0, mod_p[5], wsg, wsu, wsd, gfin)
    y_s = _final(x1_s, h2_s, routed, n_prompt, mod_s[5], wsg, wsu, wsd, gfin)

    heads = lambda z, b, s: z.reshape(1, b, s, n_heads, dh)
    pool_p = u_p[:, seq - (POOL_HALO - 1):][None]
    pool_s = ext_s[:, ext_s.shape[1] - (POOL_HALO - 1):][None]
    v_p = vt_p.reshape(bsz, n_heads, dh, seq).transpose(0, 3, 1, 2)[None]
    return (y_p, y_s.reshape(db, t_new, d), heads(k_p, bsz, seq), v_p, logf_p[None], pool_p,
            heads(k_s, db, t_new), heads(v_s, db, t_new), logf_s[None], pool_s)
```

```python
import functools

import jax
import jax.numpy as jnp
from jax import lax
from jax.experimental import pallas as pl
from jax.experimental.pallas import tpu as pltpu

F32 = jnp.float32
BF16 = jnp.bfloat16

RMS_EPS = 1e-6
LOG2E = 1.4426950408889634
TOP_K = 8
N_GROUPS = 8
TOPK_GROUPS = 4
ROUTED_SCALE = 2.5
POOL_WINDOWS = (2, 4, 8, 16)
POOL_HALO = 16
LANES = 128
DECODE_SLOTS = 3
MOE_ROWS = 128
MOE_SUBBLOCKS = 4
MOE_BLOCK = MOE_ROWS * MOE_SUBBLOCKS
VMEM_LIMIT = 56 * 1024 * 1024

_dot = functools.partial(jnp.dot, preferred_element_type=F32)


def _dot_nt(a, b, precision=None):
    return lax.dot_general(a, b, (((1,), (1,)), ((), ())), preferred_element_type=F32, precision=precision)


def _params(*sem):
    return pltpu.CompilerParams(dimension_semantics=sem, vmem_limit_bytes=VMEM_LIMIT)


def _vmem():
    return pl.BlockSpec(memory_space=pltpu.VMEM)


def _rms(x, g):
    return x * lax.rsqrt(jnp.mean(x * x, axis=-1, keepdims=True) + RMS_EPS) * g


def _rms_mod(x, g, shift, scale):
    return _rms(x, g) * (1.0 + scale) + shift


def _log_sigmoid(x):
    return jnp.minimum(x, 0.0) - jnp.log1p(jnp.exp(-jnp.abs(x)))


def _silu(x):
    return x * jax.nn.sigmoid(x)


def _cumsum_lanes(x):
    n = x.shape[-1]
    lane = lax.broadcasted_iota(jnp.int32, x.shape, x.ndim - 1)
    k = 1
    while k < n:
        x = x + jnp.where(lane >= k, pltpu.roll(x, k, axis=x.ndim - 1), 0.0)
        k *= 2
    return x


def _mod_spec(mod, tm, nt):
    _, r, d = mod.shape
    if r == 1:
        return pl.BlockSpec((1, 1, d), lambda b, i: (b, 0, 0))
    return pl.BlockSpec((1, tm, d), lambda b, i: (b, i, 0))


def _ada_kernel(c_ref, w_ref, b_ref, o_ref):
    a = _silu(c_ref[...]).astype(BF16)
    o_ref[...] = _dot(a, w_ref[...].astype(BF16)) + b_ref[...]


def _ada(c, w, b):
    r, d = c.shape
    n = w.shape[1]
    tn = min(n, 1024)
    return pl.pallas_call(
        _ada_kernel,
        grid=(n // tn,),
        in_specs=[pl.BlockSpec((r, d), lambda j: (0, 0)),
                  pl.BlockSpec((d, tn), lambda j: (0, j)),
                  pl.BlockSpec((1, tn), lambda j: (0, j))],
        out_specs=pl.BlockSpec((r, tn), lambda j: (0, j)),
        out_shape=jax.ShapeDtypeStruct((r, n), F32),
        compiler_params=_params("parallel"),
        name="ada",
    )(c, w, b.reshape(1, n))


def _inproj_kernel(x_ref, g_ref, sh_ref, sc_ref, wqt_ref, wk_ref, wvt_ref, wu_ref, wf_ref, bf_ref,
                   qt_ref, k_ref, vt_ref, u_ref, lf_ref, *, qscale):
    h = _rms_mod(x_ref[0], g_ref[...], sh_ref[0], sc_ref[0]).astype(BF16)
    qt_ref[0] = (_dot_nt(wqt_ref[...], h) * qscale).astype(BF16)
    vt_ref[0] = _dot_nt(wvt_ref[...], h)
    k_ref[0] = _dot(h, wk_ref[...])
    u_ref[0] = _dot(h, wu_ref[...])
    lf_ref[0] = _log_sigmoid(_dot(h, wf_ref[...]) + bf_ref[...])


def _inproj(x, g, shift, scale, wq_t, wk, wv_t, wu, wf, bf, qscale):
    bsz, s, d = x.shape
    a, p = wk.shape[1], wu.shape[1]
    tm = min(s, 512)
    nt = s // tm
    row = lambda n: pl.BlockSpec((1, tm, n), lambda b, i: (b, i, 0))
    col = pl.BlockSpec((1, a, tm), lambda b, i: (b, 0, i))
    return pl.pallas_call(
        functools.partial(_inproj_kernel, qscale=qscale),
        grid=(bsz, nt),
        in_specs=[row(d), pl.BlockSpec((1, d), lambda b, i: (0, 0)),
                  _mod_spec(shift, tm, nt), _mod_spec(scale, tm, nt),
                  _vmem(), _vmem(), _vmem(), _vmem(), _vmem(), _vmem()],
        out_specs=[col, row(a), col, row(p), row(LANES)],
        out_shape=[jax.ShapeDtypeStruct((bsz, a, s), BF16),
                   jax.ShapeDtypeStruct((bsz, s, a), F32),
                   jax.ShapeDtypeStruct((bsz, a, s), F32),
                   jax.ShapeDtypeStruct((bsz, s, p), F32),
                   jax.ShapeDtypeStruct((bsz, s, LANES), F32)],
        compiler_params=_params("parallel", "parallel"),
        name="inproj",
    )(x, g, shift, scale, wq_t, wk, wv_t, wu, wf, bf)


def _cumsum_kernel(x_ref, o_ref):
    o_ref[0] = _cumsum_lanes(x_ref[0])


def _cumsum_rows(x):
    bsz, h, s = x.shape
    spec = pl.BlockSpec((1, h, s), lambda b: (b, 0, 0))
    return pl.pallas_call(
        _cumsum_kernel, grid=(bsz,), in_specs=[spec], out_specs=spec,
        out_shape=jax.ShapeDtypeStruct(x.shape, F32),
        compiler_params=_params("parallel"), name="logf_cumsum",
    )(x)


def _attn_prompt_kernel(qt_ref, k_ref, vt_ref, c_ref, o_ref, m_ref, l_ref, acc_ref, *, t, dh):
    qi = pl.program_id(2)
    qt = qt_ref[0]
    row = lax.broadcasted_iota(jnp.int32, (2 * dh, 1), 0)
    zero = jnp.zeros_like(qt)
    qh = (jnp.where(row < dh, qt, zero), jnp.where(row < dh, zero, qt))
    m_ref[...] = jnp.full(m_ref.shape, -jnp.inf, F32)
    l_ref[...] = jnp.zeros(l_ref.shape, F32)
    acc_ref[...] = jnp.zeros(acc_ref.shape, F32)

    def tile(ki, diagonal):
        k0 = pl.multiple_of(ki * t, t)
        kb = k_ref[0, pl.ds(k0, t), :].astype(BF16)
        vt = vt_ref[0, :, pl.ds(k0, t)].astype(BF16)
        cb = c_ref[0, 0, ki] * LOG2E
        pv, corr = [], []
        for j in range(2):
            s = _dot(kb, qh[j]) - cb[:, j:j + 1]
            if diagonal:
                key = lax.broadcasted_iota(jnp.int32, (t, t), 0)
                qry = lax.broadcasted_iota(jnp.int32, (t, t), 1)
                s = jnp.where(key <= qry, s, -jnp.inf)
            m_old = m_ref[j]
            m_new = jnp.maximum(m_old, jnp.max(s, axis=0, keepdims=True))
            p = jnp.exp2(s - m_new)
            cj = jnp.exp2(m_old - m_new)
            l_ref[j] = l_ref[j] * cj + jnp.sum(p, axis=0, keepdims=True)
            m_ref[j] = m_new
            pv.append(_dot(vt[j * dh:(j + 1) * dh], p.astype(BF16)))
            corr.append(jnp.broadcast_to(cj, (dh, t)))
        acc_ref[...] = acc_ref[...] * jnp.concatenate(corr, axis=0) + jnp.concatenate(pv, axis=0)

    def body(ki, carry):
        tile(ki, False)
        return carry

    lax.fori_loop(0, qi, body, 0)
    tile(qi, True)
    denom = jnp.concatenate([jnp.broadcast_to(l_ref[j], (dh, t)) for j in range(2)], axis=0)
    o_ref[0] = (acc_ref[...] / denom).T.astype(o_ref.dtype)


def _attn_prompt(qt, k, vt, cum, dh):
    bsz, s, a = k.shape
    hp = a // (2 * dh)
    t = min(s, 512)
    nt = s // t
    cum = cum.reshape(bsz, hp, 2, nt, t).transpose(0, 1, 3, 4, 2)
    return pl.pallas_call(
        functools.partial(_attn_prompt_kernel, t=t, dh=dh),
        grid=(bsz, hp, nt),
        in_specs=[pl.BlockSpec((1, 2 * dh, t), lambda b, h, i: (b, h, i)),
                  pl.BlockSpec((1, s, 2 * dh), lambda b, h, i: (b, 0, h)),
                  pl.BlockSpec((1, 2 * dh, s), lambda b, h, i: (b, h, 0)),
                  pl.BlockSpec((1, 1, nt, t, 2), lambda b, h, i: (b, h, 0, 0, 0))],
        out_specs=pl.BlockSpec((1, t, 2 * dh), lambda b, h, i: (b, i, h)),
        out_shape=jax.ShapeDtypeStruct((bsz, s, a), BF16),
        scratch_shapes=[pltpu.VMEM((2, 1, t), F32), pltpu.VMEM((2, 1, t), F32),
                        pltpu.VMEM((2 * dh, t), F32)],
        compiler_params=_params("parallel", "parallel", "parallel"),
        name="attn_prompt",
    )(qt, k, vt, cum)


def _logf_pages_kernel(x_ref, rin_ref, tot_ref):
    x = x_ref[...]
    n = x.shape[1]
    key = lax.broadcasted_iota(jnp.int32, (n, 2 * n), 0)
    out = lax.broadcasted_iota(jnp.int32, (n, 2 * n), 1)
    sel = jnp.where((key > out) | (out >= n), 1.0, 0.0).astype(BF16)
    acc = jnp.zeros((x.shape[0], 2 * n), F32)
    rest = x
    for _ in range(3):
        part = rest.astype(BF16)
        acc = acc + _dot(part, sel)
        rest = rest - part.astype(F32)
    rin_ref[...] = acc[:, :n]
    tot_ref[...] = acc[:, n:]


def _logf_pages(lf_t):
    n_pool, n_heads, page = lf_t.shape
    pb = 64 if n_pool % 64 == 0 else n_pool
    spec = pl.BlockSpec((pb * n_heads, page), lambda i: (i, 0))
    rin, tot = pl.pallas_call(
        _logf_pages_kernel,
        grid=(n_pool // pb,), in_specs=[spec], out_specs=[spec, spec],
        out_shape=[jax.ShapeDtypeStruct((n_pool * n_heads, page), F32)] * 2,
        compiler_params=_params("parallel"), name="logf_pages",
    )(lf_t.reshape(n_pool * n_heads, page))
    return rin.reshape(lf_t.shape), tot.reshape(lf_t.shape)


def _decode_kernel(pt_ref, qb_ref, kn_ref, vn_ref, lfn_ref, *refs, g_pages, n_pages, n_new, n_heads, dh):
    rin_refs = refs[:g_pages]
    tot_refs = refs[g_pages:2 * g_pages]
    k_hbm, v_hbm, o_ref, m_ref, l_ref, acc_ref, carry_ref, k_buf, v_buf, sems = refs[2 * g_pages:]
    step = pl.program_id(1)
    steps = pl.num_programs(1)
    t = pl.program_id(0) * steps + step
    total = pl.num_programs(0) * steps
    qb = qb_ref[0]

    def fetch(t_of, slot):
        seq, stp = t_of // steps, t_of % steps
        copies = []
        for g in range(g_pages):
            pg = pt_ref[seq * n_pages + n_pages - 1 - (stp * g_pages + g)]
            copies.append(pltpu.make_async_copy(k_hbm.at[pg], k_buf.at[slot, g], sems.at[0, slot]))
            copies.append(pltpu.make_async_copy(v_hbm.at[pg], v_buf.at[slot, g], sems.at[1, slot]))
        return copies

    @pl.when(t == 0)
    def _():
        for ahead in range(DECODE_SLOTS - 1):
            @pl.when(ahead < total)
            def _(ahead=ahead):
                for c in fetch(ahead, ahead):
                    c.start()

    @pl.when(t + DECODE_SLOTS - 1 < total)
    def _():
        for c in fetch(t + DECODE_SLOTS - 1, (t + DECODE_SLOTS - 1) % DECODE_SLOTS):
            c.start()

    @pl.when(step == 0)
    def _():
        c = _cumsum_lanes(lfn_ref[0])
        tot = c[:, LANES - 1:LANES]
        rnew = (tot - c)[:, :16]
        s = _dot_nt(qb, kn_ref[0]) + jnp.concatenate([rnew] * n_new, axis=0)
        tok = lax.broadcasted_iota(jnp.int32, s.shape, 0) // n_heads
        key = lax.broadcasted_iota(jnp.int32, s.shape, 1)
        s = jnp.where(key <= tok, s, -jnp.inf)
        m = jnp.max(s, axis=-1, keepdims=True)
        p = jnp.exp(s - m)
        m_ref[...] = m
        l_ref[...] = jnp.sum(p, axis=-1, keepdims=True)
        acc_ref[...] = _dot(p.astype(BF16), vn_ref[0])
        carry_ref[...] = jnp.broadcast_to(tot, carry_ref.shape)

    slot = t % DECODE_SLOTS
    for c in fetch(t, slot):
        c.wait()
    carry = carry_ref[...]
    scores = []
    for g in range(g_pages):
        r = carry + rin_refs[g][0]
        carry = carry + tot_refs[g][0]
        kb = k_buf[slot, g].astype(BF16)
        scores.append(_dot(qb, kb) + jnp.concatenate([r] * n_new, axis=0))
    carry_ref[...] = carry
    s = jnp.concatenate(scores, axis=1)
    m_old = m_ref[...]
    m_new = jnp.maximum(m_old, jnp.max(s, axis=-1, keepdims=True))
    p = jnp.exp(s - m_new)
    corr = jnp.exp(m_old - m_new)
    l_ref[...] = l_ref[...] * corr + jnp.sum(p, axis=-1, keepdims=True)
    m_ref[...] = m_new
    pb = p.astype(BF16)
    page = s.shape[1] // g_pages
    pv = _dot_nt(pb[:, :page], v_buf[slot, 0].astype(BF16))
    for g in range(1, g_pages):
        pv = pv + _dot_nt(pb[:, g * page:(g + 1) * page], v_buf[slot, g].astype(BF16))
    acc_ref[...] = acc_ref[...] * corr + pv

    @pl.when(step == steps - 1)
    def _():
        o = acc_ref[...] / l_ref[...]
        head = lax.broadcasted_iota(jnp.int32, o.shape, 0) % n_heads
        lane_head = lax.broadcasted_iota(jnp.int32, o.shape, 1) // dh
        o = jnp.where(head == lane_head, o, 0.0)
        o_ref[0] = jnp.sum(o.reshape(n_new, n_heads, o.shape[1]), axis=1)


def _decode_attn(q, k, v, logf, cache_kt, cache_vt, cache_logf_t, page_table, n_heads, dh):
    db, t, a = q.shape
    n_pages = page_table.shape[1]
    page = cache_kt.shape[2]
    assert page == LANES and t <= 16
    g_pages = max(g for g in (8, 4, 2, 1) if n_pages % g == 0)
    rows = t * n_heads
    rin, tot = _logf_pages(cache_logf_t)
    eye = jnp.eye(n_heads, dtype=BF16)
    qb = (q.reshape(db, t, 1, n_heads, dh) * eye[None, None, :, :, None]).reshape(db, rows, a)
    pad = lambda z: jnp.pad(z.astype(BF16), ((0, 0), (0, 16 - t), (0, 0)))
    lfn = jnp.pad(logf.transpose(0, 2, 1), ((0, 0), (0, 0), (0, LANES - t)))

    def page_idx(g):
        return lambda b, s, pt: (pt[b * n_pages + n_pages - 1 - (s * g_pages + g)], 0, 0)

    seq = lambda shape: pl.BlockSpec((1,) + shape, lambda b, s, pt: (b, 0, 0))
    hbm = pl.BlockSpec(memory_space=pl.ANY)
    in_specs = [seq((rows, a)), seq((16, a)), seq((16, a)), seq((n_heads, LANES))]
    for _ in range(2):
        in_specs += [pl.BlockSpec((1, n_heads, page), page_idx(g)) for g in range(g_pages)]
    in_specs += [hbm, hbm]
    grid_spec = pltpu.PrefetchScalarGridSpec(
        num_scalar_prefetch=1, grid=(db, n_pages // g_pages), in_specs=in_specs,
        out_specs=seq((t, a)),
        scratch_shapes=[pltpu.VMEM((rows, 1), F32), pltpu.VMEM((rows, 1), F32),
                        pltpu.VMEM((rows, a), F32), pltpu.VMEM((n_heads, page), F32),
                        pltpu.VMEM((DECODE_SLOTS, g_pages, a, page), F32),
                        pltpu.VMEM((DECODE_SLOTS, g_pages, a, page), F32),
                        pltpu.SemaphoreType.DMA((2, DECODE_SLOTS))])
    return pl.pallas_call(
        functools.partial(_decode_kernel, g_pages=g_pages, n_pages=n_pages, n_new=t, n_heads=n_heads, dh=dh),
        grid_spec=grid_spec,
        out_shape=jax.ShapeDtypeStruct((db, t, a), F32),
        compiler_params=_params("arbitrary", "arbitrary"),
        name="attn_decode",
    )(page_table.reshape(-1), qb, pad(k), pad(v), lfn, *([rin] * g_pages), *([tot] * g_pages), cache_kt, cache_vt)


def _pool_prompt_kernel(u_ref, halo_ref, o_ref, *, tm, pg):
    i = pl.program_id(1)
    u = u_ref[0]
    halo = jnp.where(i > 0, halo_ref[0], 0.0)
    ext = jnp.concatenate([halo, u], axis=0)
    pos = i * tm + lax.broadcasted_iota(jnp.int32, (tm, 1), 0)
    outs = []
    for g, w in enumerate(POOL_WINDOWS):
        r = ext[:, g * pg:(g + 1) * pg]
        k = 1
        while k < w:
            r = r + pltpu.roll(r, k, axis=0)
            k *= 2
        count = jnp.minimum(w, pos + 1).astype(F32)
        outs.append(r[POOL_HALO:] / count - u[:, g * pg:(g + 1) * pg])
    o_ref[0] = jnp.concatenate(outs, axis=-1).astype(o_ref.dtype)


def _pool_prompt(u):
    bsz, s, p = u.shape
    tm = min(s, 512)
    hb = tm // POOL_HALO
    return pl.pallas_call(
        functools.partial(_pool_prompt_kernel, tm=tm, pg=p // len(POOL_WINDOWS)),
        grid=(bsz, s // tm),
        in_specs=[pl.BlockSpec((1, tm, p), lambda b, i: (b, i, 0)),
                  pl.BlockSpec((1, POOL_HALO, p), lambda b, i: (b, jnp.maximum(i * hb - 1, 0), 0))],
        out_specs=pl.BlockSpec((1, tm, p), lambda b, i: (b, i, 0)),
        out_shape=jax.ShapeDtypeStruct(u.shape, BF16),
        compiler_params=_params("parallel", "parallel"),
        name="pool_prompt",
    )(u, u)


def _pool_sample_kernel(e_ref, o_ref, *, n_new, n_prev, pg):
    for t in range(n_new):
        e = n_prev + t
        outs = []
        for g, w in enumerate(POOL_WINDOWS):
            cols = slice(g * pg, (g + 1) * pg)
            lo = max(0, e - w + 1)
            acc = e_ref[lo, :, cols]
            for j in range(lo + 1, e + 1):
                acc = acc + e_ref[j, :, cols]
            outs.append(acc / float(min(w, e + 1)) - e_ref[e, :, cols])
        o_ref[t] = jnp.concatenate(outs, axis=-1).astype(o_ref.dtype)


def _pool_sample(ext_t, n_new):
    n_all, db, p = ext_t.shape
    return pl.pallas_call(
        functools.partial(_pool_sample_kernel, n_new=n_new, n_prev=n_all - n_new, pg=p // len(POOL_WINDOWS)),
        in_specs=[_vmem()], out_specs=_vmem(),
        out_shape=jax.ShapeDtypeStruct((n_new, db, p), BF16),
        compiler_params=pltpu.CompilerParams(vmem_limit_bytes=VMEM_LIMIT),
        name="pool_sample",
    )(ext_t)


def _merge_kernel(x_ref, g_ref, sh_ref, sc_ref, gt_ref, oa_ref, pl_ref, wpool_ref, ps_ref,
                  wpa_ref, wpb_ref, wga_ref, wgb_ref, wo_ref, o_ref):
    x = x_ref[0]
    h = _rms_mod(x, g_ref[...], sh_ref[0], sc_ref[0]).astype(BF16)
    pooled = pl_ref[0]
    n_groups, pg, _ = wpool_ref.shape
    mixed = jnp.concatenate(
        [_dot(pooled[:, g * pg:(g + 1) * pg], wpool_ref[g]) for g in range(n_groups)], axis=-1)
    o_pool = (mixed * ps_ref[...]).astype(BF16)
    y = jax.nn.sigmoid(_dot(h, wga_ref[...])) * _dot(oa_ref[0], wpa_ref[...])
    y = y + jax.nn.sigmoid(_dot(h, wgb_ref[...])) * _dot(o_pool, wpb_ref[...])
    o_ref[0] = x + gt_ref[0] * _dot(y.astype(BF16), wo_ref[...])


def _merge(x, g, shift, scale, gate, o_att, pooled, w_pool, pool_scale, w_pa, w_pb, w_ga, w_gb, w_o):
    bsz, s, d = x.shape
    tm = min(s, 256)
    nt = s // tm
    row = lambda n: pl.BlockSpec((1, tm, n), lambda b, i: (b, i, 0))
    return pl.pallas_call(
        _merge_kernel,
        grid=(bsz, nt),
        in_specs=[row(d), pl.BlockSpec((1, d), lambda b, i: (0, 0)),
                  _mod_spec(shift, tm, nt), _mod_spec(scale, tm, nt), _mod_spec(gate, tm, nt),
                  row(o_att.shape[2]), row(pooled.shape[2]),
                  _vmem(), _vmem(), _vmem(), _vmem(), _vmem(), _vmem(), _vmem()],
        out_specs=row(d),
        out_shape=jax.ShapeDtypeStruct(x.shape, F32),
        compiler_params=_params("parallel", "parallel"),
        name="merge",
    )(x, g, shift, scale, gate, o_att, pooled, w_pool, pool_scale, w_pa, w_pb, w_ga, w_gb, w_o)


def _route_kernel(x_ref, g_ref, sh_ref, sc_ref, wr_ref, rb_ref, h_ref, idx_ref, w_ref):
    h = _rms_mod(x_ref[0], g_ref[...], sh_ref[0], sc_ref[0])
    h_ref[0] = h.astype(BF16)
    n_exp = wr_ref.shape[0]
    tm = h.shape[0]
    per = n_exp // N_GROUPS
    scores = jax.nn.sigmoid(_dot_nt(wr_ref[...], h, precision=lax.Precision.HIGHEST))
    sel = scores + rb_ref[...]
    neg = -jnp.inf
    groups = [sel[g * per:(g + 1) * per] for g in range(N_GROUPS)]
    eidx = lax.broadcasted_iota(jnp.int32, (per, tm), 0)
    gscore = []
    for grp in groups:
        m1 = jnp.max(grp, axis=0, keepdims=True)
        first = jnp.min(jnp.where(grp == m1, eidx, per), axis=0, keepdims=True)
        gscore.append(m1 + jnp.max(jnp.where(eidx == first, neg, grp), axis=0, keepdims=True))
    cand = []
    for g in range(N_GROUPS):
        rank = jnp.zeros((1, tm), jnp.int32)
        for o in range(N_GROUPS):
            if o != g:
                ahead = gscore[o] >= gscore[g] if o < g else gscore[o] > gscore[g]
                rank = rank + ahead.astype(jnp.int32)
        cand.append(jnp.where(rank < TOPK_GROUPS, groups[g], neg))
    cand = jnp.concatenate(cand, axis=0)
    row = lax.broadcasted_iota(jnp.int32, (n_exp, tm), 0)
    idxs, wts = [], []
    for _ in range(TOP_K):
        m = jnp.max(cand, axis=0, keepdims=True)
        pick = jnp.min(jnp.where(cand == m, row, n_exp), axis=0, keepdims=True)
        hit = row == pick
        idxs.append(pick)
        wts.append(jnp.sum(jnp.where(hit, scores, 0.0), axis=0, keepdims=True))
        cand = jnp.where(hit, neg, cand)
    wts = jnp.concatenate(wts, axis=0)
    idx_ref[...] = jnp.concatenate(idxs, axis=0)
    w_ref[...] = wts / jnp.sum(wts, axis=0, keepdims=True) * ROUTED_SCALE


def _route(x, g, shift, scale, w_router_t, router_bias):
    bsz, s, d = x.shape
    tm = min(s, 256)
    nt = s // tm
    col = pl.BlockSpec((TOP_K, tm), lambda b, i: (0, b * nt + i))
    return pl.pallas_call(
        _route_kernel,
        grid=(bsz, nt),
        in_specs=[pl.BlockSpec((1, tm, d), lambda b, i: (b, i, 0)), pl.BlockSpec((1, d), lambda b, i: (0, 0)),
                  _mod_spec(shift, tm, nt), _mod_spec(scale, tm, nt), _vmem(), _vmem()],
        out_specs=[pl.BlockSpec((1, tm, d), lambda b, i: (b, i, 0)), col, col],
        out_shape=[jax.ShapeDtypeStruct(x.shape, BF16),
                   jax.ShapeDtypeStruct((TOP_K, bsz * s), jnp.int32),
                   jax.ShapeDtypeStruct((TOP_K, bsz * s), F32)],
        compiler_params=_params("parallel", "parallel"),
        name="route",
    )(x, g, shift, scale, w_router_t, router_bias)


def _moe_kernel(blk_ref, exp_ref, lo_ref, hi_ref, lead_ref, nxt_ref, par_ref, xs_ref,
                wg_hbm, wu_hbm, wd_hbm, o_ref, acc_ref, wg_f, wu_f, wd_f, wg_b, wu_b, wd_b, sems):
    i = pl.program_id(0)
    lo = lo_ref[i]
    hi = hi_ref[i]

    def fetch(expert, slot):
        return [pltpu.make_async_copy(src.at[expert], dst.at[slot], sems.at[n, slot])
                for n, (src, dst) in enumerate(((wg_hbm, wg_f), (wu_hbm, wu_f), (wd_hbm, wd_f)))]

    @pl.when(i == 0)
    def _():
        for c in fetch(exp_ref[0], 0):
            c.start()

    @pl.when(lead_ref[i] == 1)
    def _():
        slot = par_ref[i]
        nxt = nxt_ref[i]

        @pl.when(nxt >= 0)
        def _():
            for c in fetch(nxt, 1 - slot):
                c.start()

        for c in fetch(exp_ref[i], slot):
            c.wait()
        wg_b[...] = wg_f[slot].astype(BF16)
        wu_b[...] = wu_f[slot].astype(BF16)
        wd_b[...] = wd_f[slot].astype(BF16)

    for j in range(MOE_SUBBLOCKS):
        r0 = j * MOE_ROWS

        @pl.when((lo < r0 + MOE_ROWS) & (hi > r0))
        def _(j=j, r0=r0):
            rows = pl.ds(r0, MOE_ROWS)
            xb = xs_ref[rows, :]
            row = r0 + lax.broadcasted_iota(jnp.int32, (MOE_ROWS, 1), 0)
            hid = (_silu(_dot(xb, wg_b[...])) * _dot(xb, wu_b[...])).astype(BF16)
            y = jnp.where((row >= lo) & (row < hi), _dot(hid, wd_b[...]), 0.0)

            @pl.when(lo <= r0)
            def _():
                acc_ref[j] = y

            @pl.when(lo > r0)
            def _():
                acc_ref[j] = acc_ref[j] + y

            o_ref[rows, :] = acc_ref[j].astype(o_ref.dtype)


def _moe(xs, e_sorted, start, w_gate, w_up, w_down):
    n_assign, d = xs.shape
    n_exp, _, f = w_gate.shape
    n_blocks = -(-n_assign // MOE_BLOCK)
    cuts = jnp.sort(jnp.concatenate([jnp.arange(n_blocks, dtype=jnp.int32) * MOE_BLOCK, start[1:]]))
    n_items = cuts.shape[0]
    ends = jnp.concatenate([cuts[1:], jnp.full((1,), n_assign, jnp.int32)])
    blk = jnp.minimum(cuts // MOE_BLOCK, n_blocks - 1)
    expert = e_sorted[jnp.minimum(cuts, n_assign - 1)]
    lo, hi = cuts - blk * MOE_BLOCK, ends - blk * MOE_BLOCK
    live = ends > cuts
    item = jnp.arange(n_items, dtype=jnp.int32)
    change = jnp.concatenate([jnp.ones((1,), bool), expert[1:] != expert[:-1]])
    seen = jnp.cumsum(live.astype(jnp.int32))
    seen_at_run_start = lax.cummax(jnp.where(change, seen - live, 0))
    lead = live & (seen - seen_at_run_start == 1)
    parity = (jnp.cumsum(lead.astype(jnp.int32)) - 1) % 2
    lead_at = jnp.where(lead, item, n_items)
    nxt_item = lax.cummin(jnp.concatenate([lead_at[1:], jnp.full((1,), n_items, jnp.int32)]), reverse=True)
    nxt = jnp.where(nxt_item < n_items, expert[jnp.minimum(nxt_item, n_items - 1)], -1)
    row = lambda n: pl.BlockSpec((MOE_BLOCK, n), lambda i, blk, *_: (blk[i], 0))
    hbm = pl.BlockSpec(memory_space=pl.ANY)
    grid_spec = pltpu.PrefetchScalarGridSpec(
        num_scalar_prefetch=7, grid=(n_items,),
        in_specs=[row(d), hbm, hbm, hbm],
        out_specs=row(d),
        scratch_shapes=[pltpu.VMEM((MOE_SUBBLOCKS, MOE_ROWS, d), F32),
                        pltpu.VMEM((2, d, f), F32), pltpu.VMEM((2, d, f), F32), pltpu.VMEM((2, f, d), F32),
                        pltpu.VMEM((d, f), BF16), pltpu.VMEM((d, f), BF16), pltpu.VMEM((f, d), BF16),
                        pltpu.SemaphoreType.DMA((3, 2))])
    i32 = lambda z: z.astype(jnp.int32)
    return pl.pallas_call(
        _moe_kernel, grid_spec=grid_spec,
        out_shape=jax.ShapeDtypeStruct((n_assign, d), BF16),
        compiler_params=_params("arbitrary"),
        name="moe",
    )(i32(blk), i32(expert), i32(lo), i32(hi), i32(lead), i32(nxt), i32(parity),
      xs, w_gate, w_up, w_down)


def _final_kernel(x_ref, h_ref, r_ref, rw_ref, gt_ref, wg_ref, wu_ref, wd_ref, gf_ref, o_ref):
    hb = h_ref[0]
    hid = (_silu(_dot(hb, wg_ref[...])) * _dot(hb, wu_ref[...])).astype(BF16)
    moe = _dot(hid, wd_ref[...])
    rw = rw_ref[...]
    for k in range(r_ref.shape[0]):
        moe = moe + r_ref[k].astype(F32) * rw[:, k:k + 1]
    o_ref[0] = _rms(x_ref[0] + gt_ref[0] * moe, gf_ref[...])


def _final(x, h, routed, route_w, row0, gate, w_sg, w_su, w_sd, g_final):
    bsz, s, d = x.shape
    tm = min(s, 256)
    nt = s // tm
    assert row0 % tm == 0
    rb0 = row0 // tm
    row = pl.BlockSpec((1, tm, d), lambda b, i: (b, i, 0))
    return pl.pallas_call(
        _final_kernel,
        grid=(bsz, nt),
        in_specs=[row, row, pl.BlockSpec((routed.shape[0], tm, d), lambda b, i: (0, rb0 + b * nt + i, 0)),
                  pl.BlockSpec((tm, route_w.shape[1]), lambda b, i: (rb0 + b * nt + i, 0)),
                  _mod_spec(gate, tm, nt), _vmem(), _vmem(), _vmem(), pl.BlockSpec((1, d), lambda b, i: (0, 0))],
        out_specs=row,
        out_shape=jax.ShapeDtypeStruct(x.shape, F32),
        compiler_params=_params("parallel", "parallel"),
        name="final",
    )(x, h, routed, route_w, gate, w_sg, w_su, w_sd, g_final)


def kernel(x_prompt, x_sample, cache_k, cache_v, cache_logf, state_pool, page_table, c_prompt, c_sample, w_ada, b_ada, g_mix, w_in, b_f, w_pool, pool_scale, w_pa, w_pb, w_o, g_ffn, w_router, router_bias, w_gate, w_up, w_down, w_sh_gate, w_sh_up, w_sh_down, g_final):
    depth = w_ada.shape[0]
    assert depth == 1
    bsz, seq, d = x_prompt.shape
    db, t_new, _ = x_sample.shape
    _, n_pool_pages, page, n_heads, dh = cache_k.shape
    a = n_heads * dh
    p = state_pool.shape[3]
    n_exp = w_router.shape[2]
    n_dec = db * t_new

    w = w_in[0]
    cuts = [a, 2 * a, 3 * a, 3 * a + n_heads, 3 * a + n_heads + p, 3 * a + n_heads + p + d]
    wq, wk, wv, wf, wu, wga, wgb = [z.astype(BF16) for z in jnp.split(w, cuts, axis=1)]
    wq_t, wv_t = wq.T, wv.T
    wf = jnp.pad(wf, ((0, 0), (0, LANES - n_heads)))
    bf = jnp.pad(b_f[0], (0, LANES - n_heads)).reshape(1, LANES)
    wpool, wpa, wpb, wo = w_pool[0].astype(BF16), w_pa[0].astype(BF16), w_pb[0].astype(BF16), w_o[0].astype(BF16)
    wsg, wsu, wsd = w_sh_gate[0].astype(BF16), w_sh_up[0].astype(BF16), w_sh_down[0].astype(BF16)
    wr_t = w_router[0].T
    rbias = router_bias[0].reshape(n_exp, 1)
    gmix, gffn, gfin = g_mix[0].reshape(1, d), g_ffn[0].reshape(1, d), g_final.reshape(1, d)
    pscale = pool_scale[0].reshape(1, p)

    c_all = jnp.concatenate([c_prompt, c_sample], axis=0)
    r_pad = -c_all.shape[0] % 16
    mod = _ada(jnp.pad(c_all, ((0, r_pad), (0, 0))), w_ada[0], b_ada[0])
    mod_p = [m.reshape(bsz, 1, d) for m in jnp.split(mod[:bsz], 6, axis=-1)]
    mod_s = [jnp.repeat(m, t_new, axis=0).reshape(1, n_dec, d) for m in jnp.split(mod[bsz:bsz + db], 6, axis=-1)]

    def mixer_in(x, m, qscale):
        return _inproj(x, gmix, m[0], m[1], wq_t, wk, wv_t, wu, wf, bf, qscale)

    def mixer_out(x, m, o_att, pooled):
        x1 = _merge(x, gmix, m[0], m[1], m[2], o_att, pooled, wpool, pscale, wpa, wpb, wga, wgb, wo)
        return (x1,) + tuple(_route(x1, gffn, m[3], m[4], wr_t, rbias))

    qt_p, k_p, vt_p, u_p, lf_p = mixer_in(x_prompt, mod_p, dh ** -0.5 * LOG2E)
    logf_p = lf_p[:, :, :n_heads]
    cum_p = _cumsum_rows(logf_p.transpose(0, 2, 1))
    o_att_p = _attn_prompt(qt_p, k_p, vt_p, cum_p, dh)
    x1_p, h2_p, idx_p, wt_p = mixer_out(x_prompt, mod_p, o_att_p, _pool_prompt(u_p))

    xs3 = x_sample.reshape(1, n_dec, d)
    qt_s, k_s, vt_s, u_s, lf_s = mixer_in(xs3, mod_s, dh ** -0.5)
    q_s, v_s = qt_s[0].T, vt_s[0].T
    logf_s = lf_s[0, :, :n_heads].reshape(db, t_new, n_heads)
    u_s = u_s.reshape(db, t_new, p)
    cache_kt = cache_k[0].transpose(0, 2, 3, 1).reshape(n_pool_pages, a, page)
    cache_vt = cache_v[0].transpose(0, 2, 3, 1).reshape(n_pool_pages, a, page)
    o_att_s = _decode_attn(q_s.reshape(db, t_new, a), k_s.reshape(db, t_new, a), v_s.reshape(db, t_new, a),
                           logf_s, cache_kt, cache_vt, cache_logf[0].transpose(0, 2, 1), page_table, n_heads, dh)
    ext_s = jnp.concatenate([state_pool[0], u_s], axis=1)
    pooled_s = _pool_sample(ext_s.transpose(1, 0, 2), t_new).transpose(1, 0, 2).reshape(1, n_dec, p)
    x1_s, h2_s, idx_s, wt_s = mixer_out(xs3, mod_s, o_att_s.astype(BF16).reshape(1, n_dec, a), pooled_s)

    n_prompt = bsz * seq
    n_tok = n_prompt + n_dec
    h2_all = jnp.concatenate([h2_p.reshape(n_prompt, d), h2_s.reshape(n_dec, d)], axis=0)
    flat_e = jnp.concatenate([idx_p, idx_s], axis=1).T.reshape(-1)
    route_w = jnp.concatenate([wt_p, wt_s], axis=1).T
    n_assign = n_tok * TOP_K
    ids = jnp.arange(n_assign, dtype=jnp.int32)
    e_sorted, order = lax.sort((flat_e, ids), num_keys=1, is_stable=True)
    _, rank = lax.sort((order, ids), num_keys=1)
    start = jnp.searchsorted(e_sorted, jnp.arange(n_exp, dtype=jnp.int32), side='left').astype(jnp.int32)
    xs = h2_all[order // TOP_K]
    ys = _moe(xs, e_sorted, start, w_gate[0], w_up[0], w_down[0])
    routed = ys[rank.reshape(n_tok, TOP_K).T.reshape(-1)].reshape(TOP_K, n_tok, d)

    y_p = _final(x1_p, h2_p, routed, route_w, 0, mod_p[5], wsg, wsu, wsd, gfin)
    y_s = _final(x1_s, h2_s, routed, route_w, n_prompt, mod_s[5], wsg, wsu, wsd, gfin)

    heads = lambda z, b, s: z.reshape(1, b, s, n_heads, dh)
    pool_p = u_p[:, seq - (POOL_HALO - 1):][None]
    pool_s = ext_s[:, ext_s.shape[1] - (POOL_HALO - 1):][None]
    v_p = vt_p.reshape(bsz, n_heads, dh, seq).transpose(0, 3, 1, 2)[None]
    return (y_p, y_s.reshape(db, t_new, d), heads(k_p, bsz, seq), v_p, logf_p[None], pool_p,
            heads(k_s, db, t_new), heads(v_s, db, t_new), logf_s[None], pool_s)
```
